```python
import jax, jax.numpy as jnp
from jax import lax
import numpy as np

D_MODEL = 2048
BATCH = 4
SEQ = 2048
DEPTH = 2

N_META = 16
NORM_EPS = 1e-5
HEAD_DIM = 64
ATT_WIDTH = D_MODEL // 2
ATT_HEADS = ATT_WIDTH // HEAD_DIM
ATT_KV_HEADS = ATT_HEADS // 4
ATT_GROUP = ATT_HEADS // ATT_KV_HEADS
Q_DIM = ATT_HEADS * HEAD_DIM
KV_DIM = ATT_KV_HEADS * HEAD_DIM
QKV_DIM = Q_DIM + 2 * KV_DIM
WINDOW = 128
BLOCK = 128
ROPE_DIM = HEAD_DIM // 4
ROPE_THETA = 500000.0
RWKV_DIM = D_MODEL - ATT_WIDTH
RWKV_HEAD_DIM = 64
RWKV_HEADS = RWKV_DIM // RWKV_HEAD_DIM
DECAY_LORA = 64
ICLR_LORA = 64
VRES_LORA = 32
GATE_LORA = 64
RWKV_IN = 3 * RWKV_DIM + DECAY_LORA + ICLR_LORA + GATE_LORA
GN_EPS = 64e-5
GATE_DIM = 2 * D_MODEL
N_IN = QKV_DIM + GATE_DIM + RWKV_IN
N_GROUPS = 8
EXPERTS_PER_GROUP = 8
N_EXPERTS = N_GROUPS * EXPERTS_PER_GROUP
TOP_K_INNER = 2
D_EXPERT = D_MODEL // 4
MOE_BLOCK = 128

kernel_name = "hybrid_swa_rwkv7_hmoe_meta"


def rms_norm(x, g):
    x32 = x.astype(jnp.float32)
    y = x32 * lax.rsqrt(jnp.mean(x32 * x32, axis=-1, keepdims=True) + NORM_EPS)
    return (y * g.astype(jnp.float32)).astype(x.dtype)


def token_shift(z):
    return jnp.pad(z, ((0, 0), (1, 0), (0, 0)))[:, :-1]


def partial_rope(x, pos):
    half = ROPE_DIM // 2
    inv = jnp.power(jnp.float32(ROPE_THETA), -jnp.arange(half, dtype=jnp.float32) / half)
    ang = pos.astype(jnp.float32)[:, None] * inv[None, :]
    cos = jnp.cos(ang)[None, :, None, :]
    sin = jnp.sin(ang)[None, :, None, :]
    xr = x[..., :ROPE_DIM].astype(jnp.float32)
    x1, x2 = xr[..., :half], xr[..., half:]
    rot = jnp.concatenate([x1 * cos - x2 * sin, x2 * cos + x1 * sin], axis=-1).astype(x.dtype)
    return jnp.concatenate([rot, x[..., ROPE_DIM:]], axis=-1)


def sliding_window_attention(q, k, v, sinks):
    B, L = q.shape[:2]
    f32 = jnp.float32
    pad = BLOCK - N_META
    nb = (L + pad) // BLOCK

    def to_blocks(t):
        t = jnp.pad(t, ((0, 0), (pad, 0)) + ((0, 0),) * (t.ndim - 2))
        return t.reshape((B, nb, BLOCK) + t.shape[2:])

    qb = to_blocks(q.reshape(B, L, ATT_KV_HEADS, ATT_GROUP, HEAD_DIM)).astype(f32)
    kb = to_blocks(k).astype(f32)
    vb = to_blocks(v)
    prev_blk = lambda t: jnp.concatenate([jnp.zeros_like(t[:, :1]), t[:, :-1]], axis=1)
    k_band = jnp.concatenate([prev_blk(kb), kb], axis=2)
    v_band = jnp.concatenate([prev_blk(vb), vb], axis=2)
    k_meta = k[:, :N_META].astype(f32)
    v_meta = v[:, :N_META]
    qpos = (jnp.arange(nb * BLOCK) - pad).reshape(nb, BLOCK)
    kpos = jnp.concatenate([qpos - BLOCK, qpos], axis=1)
    dist = qpos[:, :, None] - kpos[:, None, :]
    band_ok = (dist >= 0) & (dist < WINDOW) & (kpos[:, None, :] >= N_META)
    meta_ok = jnp.arange(N_META)[None, None, :] <= qpos[:, :, None]
    scale = HEAD_DIM ** -0.5
    s_meta = jnp.einsum('bnqhgd,bmhd->bnhgqm', qb, k_meta) * scale
    s_band = jnp.einsum('bnqhgd,bnshd->bnhgqs', qb, k_band) * scale
    neg = jnp.float32(-1e30)
    s_meta = jnp.where(meta_ok[None, :, None, None], s_meta, neg)
    s_band = jnp.where(band_ok[None, :, None, None], s_band, neg)
    sink = jnp.broadcast_to(
        sinks.astype(f32).reshape(ATT_KV_HEADS, ATT_GROUP)[None, None, :, :, None, None],
        s_meta.shape[:-1] + (1,))
    p = jax.nn.softmax(jnp.concatenate([s_meta, s_band, sink], axis=-1), axis=-1).astype(v.dtype)
    o = (jnp.einsum('bnhgqm,bmhd->bnqhgd', p[..., :N_META], v_meta)
         + jnp.einsum('bnhgqs,bnshd->bnqhgd', p[..., N_META:N_META + 2 * BLOCK], v_band))
    return o.reshape(B, nb * BLOCK, Q_DIM)[:, pad:]


def rwkv7_recurrence(r, decay, k, v, a_vec, b_vec):
    B, L, H, N = r.shape

    def step(S, inp):
        r_t, w_t, k_t, v_t, a_t, b_t = inp
        Sa = jnp.einsum('bhij,bhj->bhi', S, a_t)
        S = S * w_t[:, :, None, :] + Sa[..., None] * b_t[:, :, None, :] + v_t[..., None] * k_t[:, :, None, :]
        return S, jnp.einsum('bhij,bhj->bhi', S, r_t)

    xs = tuple(jnp.moveaxis(t, 1, 0) for t in (r, decay, k, v, a_vec, b_vec))
    _, ys = lax.scan(step, jnp.zeros((B, H, N, N), jnp.float32), xs)
    return jnp.moveaxis(ys, 0, 1)


def rwkv7_time_mix(z, u, v_first, mu, w0, w_decay_up, a0, w_iclr_up, w_gate_up,
                   k_k, k_a, r_k, ln_w, ln_b, vres):
    B, L, _ = z.shape
    H, N, C = RWKV_HEADS, RWKV_HEAD_DIM, RWKV_DIM
    f32 = jnp.float32
    zs = z + (token_shift(z) - z) * mu
    r, k, v = zs[..., :C], zs[..., C:2 * C], zs[..., 2 * C:3 * C]
    o = 3 * C
    wd = zs[..., o:o + DECAY_LORA]
    o += DECAY_LORA
    ad = zs[..., o:o + ICLR_LORA]
    o += ICLR_LORA
    gd = zs[..., o:o + GATE_LORA]
    w_log = -jax.nn.softplus(-(w0 + jnp.tanh(wd) @ w_decay_up).astype(f32)) - 0.5
    decay = jnp.exp(-jnp.exp(w_log))
    a = jax.nn.sigmoid(a0 + ad @ w_iclr_up)
    g = jax.nn.sigmoid(gd) @ w_gate_up
    if vres is None:
        v_first = v
    else:
        v_down, v_mu, v0, v_up = vres
        s = u @ v_down
        s = s + (token_shift(s) - s) * v_mu
        v = v + (v_first - v) * jax.nn.sigmoid(v0 + s @ v_up)
    heads = lambda t: t.reshape(B, L, H, N).astype(f32)
    kk = heads(k * k_k)
    kk = kk / jnp.maximum(jnp.sqrt(jnp.sum(kk * kk, axis=-1, keepdims=True)), 1e-12)
    k = k * (1 + (a - 1) * k_a)
    rh, kh, vh, ah = heads(r), heads(k), heads(v), heads(a)
    y = rwkv7_recurrence(rh, heads(decay), kh, vh, -kk, kk * ah)
    mean = jnp.mean(y, axis=-1, keepdims=True)
    var = jnp.mean(jnp.square(y - mean), axis=-1, keepdims=True)
    y = (y - mean) * lax.rsqrt(var + GN_EPS)
    y = y * ln_w.astype(f32).reshape(H, N) + ln_b.astype(f32).reshape(H, N)
    y = y + jnp.sum(rh * kh * r_k.astype(f32), axis=-1, keepdims=True) * vh
    return y.reshape(B, L, C).astype(z.dtype) * g, v_first


def hierarchical_moe(h, w_coarse, b_coarse, w_fine, b_fine, w_gate, w_up, w_down):
    B, L, D = h.shape
    T = B * L
    f32 = jnp.float32
    x = h.reshape(T, D)
    cl = (x @ w_coarse).astype(f32) + b_coarse.astype(f32)
    c_val, c_idx = lax.top_k(cl, 1)
    grp = c_idx[:, 0]
    p_grp = jnp.exp(c_val[:, 0] - jax.nn.logsumexp(cl, axis=-1))
    fl = jnp.einsum('td,dge->tge', x, w_fine).astype(f32) + b_fine.astype(f32)
    fl = jnp.take_along_axis(fl, grp[:, None, None], axis=1)[:, 0]
    f_val, f_idx = lax.top_k(fl, TOP_K_INNER)
    wts = jax.nn.softmax(f_val, axis=-1) * p_grp[:, None]
    expert = grp[:, None] * EXPERTS_PER_GROUP + f_idx

    A = T * TOP_K_INNER
    e_flat = expert.reshape(A)
    tok_flat = jnp.repeat(jnp.arange(T, dtype=jnp.int32), TOP_K_INNER)
    w_flat = wts.reshape(A)
    order = jnp.argsort(e_flat)
    e_s, tok_s, w_s = e_flat[order], tok_flat[order], w_flat[order]
    counts = jnp.zeros((N_EXPERTS,), jnp.int32).at[e_flat].add(1)
    padded = ((counts + MOE_BLOCK - 1) // MOE_BLOCK) * MOE_BLOCK
    start = jnp.cumsum(counts) - counts
    pend = jnp.cumsum(padded)
    pstart = pend - padded
    dest = pstart[e_s] + jnp.arange(A, dtype=jnp.int32) - start[e_s]
    n_blocks = -(-(A + N_EXPERTS * (MOE_BLOCK - 1)) // MOE_BLOCK)
    P = n_blocks * MOE_BLOCK
    row_tok = jnp.full((P,), T, jnp.int32).at[dest].set(tok_s)
    row_w = jnp.zeros((P,), f32).at[dest].set(w_s)
    blk_start = jnp.arange(n_blocks, dtype=jnp.int32) * MOE_BLOCK
    blk_expert = jnp.minimum(jnp.sum(blk_start[:, None] >= pend[None, :], axis=1), N_EXPERTS - 1)
    x_pad = jnp.concatenate([x, jnp.zeros((1, D), x.dtype)], axis=0)
    x_rows = x_pad[row_tok].reshape(n_blocks, MOE_BLOCK, D)

    def expert_block(args):
        xb, e = args
        hid = jax.nn.silu(xb @ w_gate[e]) * (xb @ w_up[e])
        return hid @ w_down[e]

    y_rows = lax.map(expert_block, (x_rows, blk_expert)).reshape(P, D)
    out = jax.ops.segment_sum(y_rows.astype(f32) * row_w[:, None], row_tok, num_segments=T + 1)[:T]
    return out.reshape(B, L, D).astype(h.dtype)


def setup_inputs(seed: int = 0) -> dict:
    key = jax.random.key(seed)
    keys = jax.random.split(key, 40)
    it = iter(range(40))
    nxt = lambda: keys[next(it)]
    f32 = jnp.float32
    nrm = lambda shape, scale: jax.random.normal(nxt(), shape, f32) * scale
    uni = lambda shape: jax.random.uniform(nxt(), shape, f32)
    D, C, Dl, Dr = D_MODEL, RWKV_DIM, DEPTH, DEPTH - 1
    return {
        "x": nrm((BATCH, SEQ, D), 1.0),
        "meta_tokens": nrm((N_META, D), 1.0),
        "norm_mix": 1.0 + nrm((Dl, D), 0.02),
        "w_in": nrm((Dl, D, N_IN), D ** -0.5),
        "b_qkv": nrm((Dl, QKV_DIM), 0.02),
        "sinks": nrm((Dl, ATT_HEADS), 0.5),
        "rwkv_mu": uni((Dl, RWKV_IN)),
        "rwkv_w0": jnp.linspace(-6.0, -1.0, C, dtype=f32)[None, :] + nrm((Dl, C), 0.1),
        "rwkv_w_decay_up": nrm((Dl, DECAY_LORA, C), 0.1 * DECAY_LORA ** -0.5),
        "rwkv_a0": nrm((Dl, C), 0.1),
        "rwkv_w_iclr_up": nrm((Dl, ICLR_LORA, C), ICLR_LORA ** -0.5),
        "rwkv_w_gate_up": nrm((Dl, GATE_LORA, C), GATE_LORA ** -0.5),
        "rwkv_k_k": 0.85 + nrm((Dl, C), 0.02),
        "rwkv_k_a": 1.0 + nrm((Dl, C), 0.02),
        "rwkv_r_k": nrm((Dl, RWKV_HEADS, RWKV_HEAD_DIM), 0.1),
        "rwkv_ln_w": 1.0 + nrm((Dl, C), 0.02),
        "rwkv_ln_b": nrm((Dl, C), 0.02),
        "rwkv_vres_down": nrm((Dr, D, VRES_LORA), D ** -0.5),
        "rwkv_vres_mu": uni((Dr, VRES_LORA)),
        "rwkv_v0": 1.0 + nrm((Dr, C), 0.1),
        "rwkv_vres_up": nrm((Dr, VRES_LORA, C), VRES_LORA ** -0.5),
        "w_branch_att": nrm((Dl, Q_DIM, D), Q_DIM ** -0.5),
        "w_branch_rwkv": nrm((Dl, C, D), C ** -0.5),
        "w_out": nrm((Dl, D, D), D ** -0.5),
        "norm_ffn": 1.0 + nrm((Dl, D), 0.02),
        "w_coarse": nrm((Dl, D, N_GROUPS), D ** -0.5),
        "b_coarse": nrm((Dl, N_GROUPS), 0.01),
        "w_fine": nrm((Dl, D, N_GROUPS, EXPERTS_PER_GROUP), D ** -0.5),
        "b_fine": nrm((Dl, N_GROUPS, EXPERTS_PER_GROUP), 0.01),
        "w_exp_gate": nrm((Dl, N_EXPERTS, D, D_EXPERT), D ** -0.5),
        "w_exp_up": nrm((Dl, N_EXPERTS, D, D_EXPERT), D ** -0.5),
        "w_exp_down": nrm((Dl, N_EXPERTS, D_EXPERT, D), D_EXPERT ** -0.5),
        "norm_final": 1.0 + nrm((D,), 0.02),
    }


def reference(x, meta_tokens, norm_mix, w_in, b_qkv, sinks, rwkv_mu, rwkv_w0, rwkv_w_decay_up,
              rwkv_a0, rwkv_w_iclr_up, rwkv_w_gate_up, rwkv_k_k, rwkv_k_a, rwkv_r_k, rwkv_ln_w,
              rwkv_ln_b, rwkv_vres_down, rwkv_vres_mu, rwkv_v0, rwkv_vres_up, w_branch_att,
              w_branch_rwkv, w_out, norm_ffn, w_coarse, b_coarse, w_fine, b_fine, w_exp_gate,
              w_exp_up, w_exp_down, norm_final):
    B = x.shape[0]
    meta = jnp.broadcast_to(meta_tokens.astype(x.dtype)[None], (B, N_META, D_MODEL))
    h = jnp.concatenate([meta, x], axis=1)
    L = h.shape[1]
    pos = jnp.arange(L)
    v_first = None
    for l in range(DEPTH):
        u = rms_norm(h, norm_mix[l])
        z = u @ w_in[l]
        z_qkv = z[..., :QKV_DIM] + b_qkv[l]
        z_gate = z[..., QKV_DIM:QKV_DIM + GATE_DIM]
        z_rwkv = z[..., QKV_DIM + GATE_DIM:]
        q = partial_rope(z_qkv[..., :Q_DIM].reshape(B, L, ATT_HEADS, HEAD_DIM), pos)
        k = partial_rope(z_qkv[..., Q_DIM:Q_DIM + KV_DIM].reshape(B, L, ATT_KV_HEADS, HEAD_DIM), pos)
        v = z_qkv[..., Q_DIM + KV_DIM:].reshape(B, L, ATT_KV_HEADS, HEAD_DIM)
        o_att = sliding_window_attention(q, k, v, sinks[l])
        vres = None if l == 0 else (rwkv_vres_down[l - 1], rwkv_vres_mu[l - 1],
                                    rwkv_v0[l - 1], rwkv_vres_up[l - 1])
        o_rwkv, v_first = rwkv7_time_mix(z_rwkv, u, v_first, rwkv_mu[l], rwkv_w0[l],
                                         rwkv_w_decay_up[l], rwkv_a0[l], rwkv_w_iclr_up[l],
                                         rwkv_w_gate_up[l], rwkv_k_k[l], rwkv_k_a[l], rwkv_r_k[l],
                                         rwkv_ln_w[l], rwkv_ln_b[l], vres)
        g_att = jax.nn.sigmoid(z_gate[..., :D_MODEL])
        g_rwkv = jax.nn.sigmoid(z_gate[..., D_MODEL:])
        mixed = g_att * (o_att @ w_branch_att[l]) + g_rwkv * (o_rwkv @ w_branch_rwkv[l])
        h = h + mixed @ w_out[l]
        h = h + hierarchical_moe(rms_norm(h, norm_ffn[l]), w_coarse[l], b_coarse[l], w_fine[l],
                                 b_fine[l], w_exp_gate[l], w_exp_up[l], w_exp_down[l])
    h = rms_norm(h, norm_final)
    return h[:, N_META:]
```

```python
import functools

import jax
import jax.numpy as jnp
from jax import lax
from jax.experimental import pallas as pl
from jax.experimental.pallas import tpu as pltpu

F32 = jnp.float32
BF16 = jnp.bfloat16

D_MODEL = 2048
N_META = 16
NORM_EPS = 1e-5
HEAD_DIM = 64
ATT_HEADS = 16
ATT_KV_HEADS = 4
ATT_GROUP = ATT_HEADS // ATT_KV_HEADS
Q_DIM = ATT_HEADS * HEAD_DIM
KV_DIM = ATT_KV_HEADS * HEAD_DIM
QKV_DIM = Q_DIM + 2 * KV_DIM
BLOCK = 128
ROPE_DIM = HEAD_DIM // 4
ROPE_HALF = ROPE_DIM // 2
ROPE_THETA = 500000.0
RWKV_DIM = 1024
RWKV_HEADS = 16
LORA = 64
VRES_LORA = 32
GN_EPS = 64e-5
GATE_DIM = 2 * D_MODEL
N_GROUPS = 8
EXPERTS_PER_GROUP = 8
N_EXPERTS = N_GROUPS * EXPERTS_PER_GROUP
D_EXPERT = D_MODEL // 4
MOE_BLOCK = 128

PAD = BLOCK - N_META
LANES = 128
CHUNK = 64
N_PAIRS = RWKV_HEADS // 2
RKV_COLS = 3 * RWKV_DIM
LORA_COLS = 256
RWKV_COLS = RKV_COLS + LORA_COLS
ROUTE_COLS = 128
VMEM_LIMIT = 56 * 1024 * 1024


def _cparams(sem):
    return pltpu.CompilerParams(dimension_semantics=sem, vmem_limit_bytes=VMEM_LIMIT)


def _dot(a, b):
    return jnp.dot(a.astype(BF16), b.astype(BF16), preferred_element_type=F32)


def _dot_nt(a, b):
    return lax.dot_general(a.astype(BF16), b.astype(BF16), (((1,), (1,)), ((), ())),
                           preferred_element_type=F32)


def _sigmoid(x):
    return 1.0 / (1.0 + jnp.exp(-x))


def _rmsnorm_kernel(h_ref, g_ref, o_ref):
    x = h_ref[...]
    y = x * lax.rsqrt(jnp.mean(x * x, axis=-1, keepdims=True) + NORM_EPS)
    o_ref[...] = (y * g_ref[...]).astype(o_ref.dtype)


def _rmsnorm(h, g, tm, out_dtype):
    t, d = h.shape
    return pl.pallas_call(
        _rmsnorm_kernel,
        out_shape=jax.ShapeDtypeStruct((t, d), out_dtype),
        grid=(t // tm,),
        in_specs=[pl.BlockSpec((tm, d), lambda i: (i, 0)),
                  pl.BlockSpec((1, d), lambda i: (0, 0))],
        out_specs=pl.BlockSpec((tm, d), lambda i: (i, 0)),
        compiler_params=_cparams(("parallel",)),
        name="rmsnorm",
    )(h, g.reshape(1, d))


def _proj_kernel(u_ref, w_ref, b_ref, o_ref, *, act):
    z = jnp.dot(u_ref[...], w_ref[...], preferred_element_type=F32) + b_ref[...]
    if act == "sigmoid":
        z = _sigmoid(z)
    o_ref[...] = z.astype(o_ref.dtype)


def _proj(u, w, b, tm, act, out_dtype, name):
    t, d = u.shape
    n = w.shape[1]
    tn = 512 if n % 512 == 0 else 256
    return pl.pallas_call(
        functools.partial(_proj_kernel, act=act),
        out_shape=jax.ShapeDtypeStruct((t, n), out_dtype),
        grid=(t // tm, n // tn),
        in_specs=[pl.BlockSpec((tm, d), lambda i, j: (i, 0)),
                  pl.BlockSpec((d, tn), lambda i, j: (0, j)),
                  pl.BlockSpec((1, tn), lambda i, j: (0, j))],
        out_specs=pl.BlockSpec((tm, tn), lambda i, j: (i, j)),
        compiler_params=_cparams(("parallel", "arbitrary")),
        name=name,
    )(u, w, b.reshape(1, n))


def _rope(x, tab_ref):
    cos, sin_lo, sin_hi = tab_ref[0], tab_ref[1], tab_ref[2]
    outs = []
    for c in range(x.shape[1] // LANES):
        xc = x[:, c * LANES:(c + 1) * LANES]
        up = pltpu.roll(xc, LANES - ROPE_HALF, axis=1)
        dn = pltpu.roll(xc, ROPE_HALF, axis=1)
        outs.append(xc * cos + up * sin_lo + dn * sin_hi)
    return jnp.concatenate(outs, axis=1)


def _attn_kernel(sink_ref, q_ref, kc_ref, kp_ref, km_ref, vc_ref, vp_ref, vm_ref,
                 tc_ref, tp_ref, tm_ref, o_ref):
    n = pl.program_id(1)
    q = _rope(q_ref[...], tc_ref) * (HEAD_DIM ** -0.5)
    keys = jnp.concatenate([_rope(kp_ref[...], tp_ref), _rope(kc_ref[...], tc_ref),
                            _rope(km_ref[...], tm_ref)], axis=0).astype(BF16)
    vals = jnp.concatenate([vp_ref[...], vc_ref[...], vm_ref[...]], axis=0).astype(BF16)
    q = q.astype(BF16)

    rows = ATT_GROUP * BLOCK
    r = lax.broadcasted_iota(jnp.int32, (rows, 3 * BLOCK), 0) & (BLOCK - 1)
    c = lax.broadcasted_iota(jnp.int32, (rows, 3 * BLOCK), 1)
    band = (c > r) & (c <= r + BLOCK) & (c + BLOCK * n >= 2 * BLOCK)
    m = c - 2 * BLOCK
    meta = (m >= PAD) & (m <= r + n * BLOCK)
    ok = ((c < 2 * BLOCK) & band) | ((c >= 2 * BLOCK) & meta)
    g_of_row = lax.broadcasted_iota(jnp.int32, (rows, 1), 0) >> (BLOCK.bit_length() - 1)

    outs = []
    for h in range(ATT_KV_HEADS):
        qg = jnp.concatenate(
            [q[:, (h * ATT_GROUP + g) * HEAD_DIM:(h * ATT_GROUP + g + 1) * HEAD_DIM]
             for g in range(ATT_GROUP)], axis=0)
        kh = keys[:, h * HEAD_DIM:(h + 1) * HEAD_DIM]
        vh = vals[:, h * HEAD_DIM:(h + 1) * HEAD_DIM]
        s = lax.dot_general(qg, kh, (((1,), (1,)), ((), ())), preferred_element_type=F32)
        s = jnp.where(ok, s, -1e30)
        sink = jnp.zeros((rows, 1), F32)
        for g in range(ATT_GROUP):
            sink = jnp.where(g_of_row == g, sink_ref[h * ATT_GROUP + g], sink)
        mx = jnp.maximum(jnp.max(s, axis=-1, keepdims=True), sink)
        e = jnp.exp(s - mx)
        den = jnp.sum(e, axis=-1, keepdims=True) + jnp.exp(sink - mx)
        p = (e / den).astype(BF16)
        og = jnp.dot(p, vh, preferred_element_type=F32)
        outs.extend(og[g * BLOCK:(g + 1) * BLOCK] for g in range(ATT_GROUP))
    o_ref[...] = jnp.concatenate(outs, axis=1).astype(o_ref.dtype)


def _rope_tables(lp):
    pos = (jnp.arange(lp) - PAD).astype(F32)
    inv = jnp.power(jnp.float32(ROPE_THETA), -jnp.arange(ROPE_HALF, dtype=F32) / ROPE_HALF)
    ang = pos[:, None] * inv[None, :]
    cos, sin = jnp.cos(ang), jnp.sin(ang)
    zeros = jnp.zeros((lp, HEAD_DIM - ROPE_DIM), F32)
    zh = jnp.zeros((lp, ROPE_HALF), F32)
    c_tab = jnp.concatenate([cos, cos, jnp.ones_like(zeros)], axis=1)
    lo_tab = jnp.concatenate([-sin, zh, zeros], axis=1)
    hi_tab = jnp.concatenate([zh, sin, zeros], axis=1)
    tab = jnp.stack([c_tab, lo_tab, hi_tab])
    return jnp.tile(tab, (1, 1, LANES // HEAD_DIM))


def _attention(zqkv, sinks, tables, batch, lp):
    nb = lp // BLOCK
    qcols = Q_DIM // KV_DIM
    cur = lambda b, n: b * nb + n
    prev = lambda b, n: b * nb + jnp.maximum(n - 1, 0)
    first = lambda b, n: b * nb
    kv_spec = lambda rowf, col: pl.BlockSpec((BLOCK, KV_DIM), lambda b, n: (rowf(b, n), col))
    tab_spec = lambda f: pl.BlockSpec((3, BLOCK, LANES), lambda b, n: (0, f(b, n), 0))
    return pl.pallas_call(
        _attn_kernel,
        out_shape=jax.ShapeDtypeStruct((batch * lp, Q_DIM), BF16),
        grid=(batch, nb),
        in_specs=[pl.BlockSpec(memory_space=pltpu.SMEM),
                  pl.BlockSpec((BLOCK, Q_DIM), lambda b, n: (cur(b, n), 0)),
                  kv_spec(cur, qcols), kv_spec(prev, qcols), kv_spec(first, qcols),
                  kv_spec(cur, qcols + 1), kv_spec(prev, qcols + 1), kv_spec(first, qcols + 1),
                  tab_spec(lambda b, n: n), tab_spec(lambda b, n: jnp.maximum(n - 1, 0)),
                  tab_spec(lambda b, n: 0)],
        out_specs=pl.BlockSpec((BLOCK, Q_DIM), lambda b, n: (cur(b, n), 0)),
        compiler_params=_cparams(("parallel", "arbitrary")),
        name="swa_attention",
    )(sinks, zqkv, zqkv, zqkv, zqkv, zqkv, zqkv, zqkv, tables, tables, tables)


def _block_diag(p):
    lane = lax.broadcasted_iota(jnp.int32, p.shape, 1)
    return jnp.concatenate([jnp.where(lane < HEAD_DIM, p, 0.0),
                            jnp.where(lane >= HEAD_DIM, p, 0.0)], axis=0)


def _head_sum(x, ones_bd):
    hi = x.astype(BF16)
    lo = (x - hi.astype(F32)).astype(BF16)
    return (jnp.dot(hi, ones_bd, preferred_element_type=F32)
            + jnp.dot(lo, ones_bd, preferred_element_type=F32))


def _softplus(x):
    return jnp.maximum(x, 0.0) + jnp.log(1.0 + jnp.exp(-jnp.abs(x)))


def _rwkv_kernel(*refs, has_vres):
    if has_vres:
        (z_ref, zprev_ref, mu_ref, vec_ref, wd_ref, wa_ref, wg_ref, wv_ref, vfirst_ref,
         o_ref, h_scr) = refs
    else:
        (z_ref, zprev_ref, mu_ref, vec_ref, wd_ref, wa_ref, wg_ref,
         o_ref, vfirst_out_ref, h_scr) = refs
    c = pl.program_id(1)

    @pl.when(c == 0)
    def _():
        h_scr[...] = jnp.zeros_like(h_scr)

    z = z_ref[...]
    row = lax.broadcasted_iota(jnp.int32, z.shape, 0)
    prev_row = zprev_ref[7:8, :] * jnp.where(c > 0, 1.0, 0.0)
    z_shift = jnp.where(row == 0, prev_row, pltpu.roll(z, 1, axis=0))
    zs = z + (z_shift - z) * mu_ref[...]

    w0, a0, k_k, k_a, r_k, ln_w, ln_b, v0 = (vec_ref[i:i + 1, :] for i in range(8))
    r = zs[:, :RWKV_DIM]
    k = zs[:, RWKV_DIM:2 * RWKV_DIM]
    v = zs[:, 2 * RWKV_DIM:RKV_COLS]
    lora = zs[:, RKV_COLS:]
    w_log = -_softplus(-(w0 + _dot(jnp.tanh(lora), wd_ref[...]))) - 0.5
    lw = -jnp.exp(w_log)
    a_sig = _sigmoid(a0 + _dot(lora, wa_ref[...]))
    gate = _dot(_sigmoid(lora), wg_ref[...])
    if has_vres:
        v = v + (vfirst_ref[...] - v) * _sigmoid(v0 + _dot(lora, wv_ref[...]))
    else:
        vfirst_out_ref[...] = v

    lane2 = lax.broadcasted_iota(jnp.int32, (LANES, LANES), 1) >= HEAD_DIM
    row2 = lax.broadcasted_iota(jnp.int32, (LANES, LANES), 0) >= HEAD_DIM
    same_head = lane2 == row2
    ones_bd = same_head.astype(BF16)

    def per_pair(fn, *xs):
        return jnp.concatenate(
            [fn(*(x[:, p * LANES:(p + 1) * LANES] for x in xs)) for p in range(N_PAIRS)], axis=1)

    kk = k * k_k
    ssq = per_pair(lambda x: _head_sum(x * x, ones_bd), kk)
    kk = kk / jnp.maximum(jnp.sqrt(ssq), 1e-12)
    k = k * (1.0 + (a_sig - 1.0) * k_a)
    a_vec = -kk
    b_vec = kk * a_sig

    trow = lax.broadcasted_iota(jnp.int32, lw.shape, 0)
    cum = lw
    sh = 1
    while sh < CHUNK:
        cum = cum + jnp.where(trow >= sh, pltpu.roll(cum, sh, axis=0), 0.0)
        sh *= 2
    tot = cum[CHUNK - 1:CHUNK, :]
    e_pos = jnp.exp(cum)
    e_neg = jnp.exp(-cum)
    e_tot = jnp.exp(tot)
    a_t = a_vec * jnp.exp(cum - lw)
    r_t = r * e_pos
    b_t = b_vec * e_neg
    k_t = k * e_neg
    b_h = b_t * e_tot
    k_h = k_t * e_tot

    t_idx = lax.broadcasted_iota(jnp.int32, (CHUNK, LANES), 0)
    s_idx = lax.broadcasted_iota(jnp.int32, (CHUNK, LANES), 1) & (HEAD_DIM - 1)
    strict = t_idx > s_idx
    incl = t_idx >= s_idx
    eye = (t_idx == s_idx).astype(F32)
    diag_mask = (lax.broadcasted_iota(jnp.int32, (LANES, LANES), 0)
                 == lax.broadcasted_iota(jnp.int32, (LANES, LANES), 1))

    sl = lambda x, p: x[:, p * LANES:(p + 1) * LANES]
    pairs = range(N_PAIRS)
    at = [sl(a_t, p) for p in pairs]
    rt = [sl(r_t, p) for p in pairs]
    vp = [sl(v, p) for p in pairs]

    a_ab, a_ak, m_rb, m_rk = [], [], [], []
    for p in pairs:
        lhs = jnp.concatenate([at[p], rt[p]], axis=0)
        rhs = jnp.concatenate([_block_diag(sl(b_t, p)), _block_diag(sl(k_t, p))], axis=0)
        sc = _dot_nt(lhs, rhs)
        a_ab.append(jnp.where(strict, sc[:CHUNK, :LANES], 0.0))
        a_ak.append(jnp.where(strict, sc[:CHUNK, LANES:], 0.0))
        m_rb.append(jnp.where(incl, sc[CHUNK:, :LANES], 0.0))
        m_rk.append(jnp.where(incl, sc[CHUNK:, LANES:], 0.0))

    pw = [_dot(a_ab[p], _block_diag(a_ab[p])) for p in pairs]
    tm = [eye + a_ab[p] for p in pairs]
    n_stage = CHUNK.bit_length() - 2
    for stage in range(n_stage):
        last = stage == n_stage - 1
        for p in pairs:
            if last:
                tm[p] = tm[p] + _dot(pw[p], _block_diag(tm[p]))
            else:
                res = _dot(pw[p], jnp.concatenate([_block_diag(pw[p]), _block_diag(tm[p])], axis=1))
                pw[p] = res[:, :LANES]
                tm[p] = tm[p] + res[:, LANES:]

    outs = []
    for p in pairs:
        x = _dot(a_ak[p], _block_diag(vp[p]))
        wu = _dot(tm[p], jnp.concatenate([_block_diag(at[p]), _block_diag(x)], axis=1))
        w_m, u0 = wu[:, :LANES], wu[:, LANES:]
        qo = _dot(m_rb[p], jnp.concatenate([_block_diag(w_m), _block_diag(u0)], axis=1))
        q_hat = rt[p] + qo[:, :LANES]
        o0 = qo[:, LANES:] + _dot(m_rk[p], _block_diag(vp[p]))
        gh = _dot(sl(b_h, p).T, wu)
        kv = _dot(sl(k_h, p).T, vp[p])
        g_m = jnp.where(same_head, gh[:, :LANES], 0.0) + jnp.where(diag_mask, sl(e_tot, p), 0.0)
        h_add = jnp.where(same_head, gh[:, LANES:] + kv, 0.0)
        ser = _dot(jnp.concatenate([q_hat, g_m], axis=0), h_scr[p])
        outs.append(ser[:CHUNK] + o0)
        h_scr[p] = ser[CHUNK:] + h_add
    y = jnp.concatenate(outs, axis=1)

    inv_n = 1.0 / HEAD_DIM
    mean = per_pair(lambda x: _head_sum(x, ones_bd), y) * inv_n
    yc = y - mean
    var = per_pair(lambda x: _head_sum(x * x, ones_bd), yc) * inv_n
    yn = yc * lax.rsqrt(var + GN_EPS) * ln_w + ln_b
    bonus = per_pair(lambda x: _head_sum(x, ones_bd), r * k * r_k)
    o_ref[...] = ((yn + bonus * v) * gate).astype(o_ref.dtype)


def _rwkv(z_r, mu, vecs, wd, wa, wg, wv, v_first, batch, lp):
    nc = lp // CHUNK
    has_vres = v_first is not None
    row_blk = lambda b, c: b * nc + c
    const = lambda shape: pl.BlockSpec(shape, lambda b, c: (0,) * len(shape))
    in_specs = [pl.BlockSpec((CHUNK, RWKV_COLS), lambda b, c: (row_blk(b, c), 0)),
                pl.BlockSpec((8, RWKV_COLS),
                             lambda b, c: (jnp.maximum(row_blk(b, c) * (CHUNK // 8) - 1, 0), 0)),
                const((1, RWKV_COLS)), const((8, RWKV_DIM)),
                const((LORA_COLS, RWKV_DIM)), const((LORA_COLS, RWKV_DIM)),
                const((LORA_COLS, RWKV_DIM))]
    args = [z_r, z_r, mu, vecs, wd, wa, wg]
    chunk_spec = pl.BlockSpec((CHUNK, RWKV_DIM), lambda b, c: (row_blk(b, c), 0))
    o_shape = jax.ShapeDtypeStruct((batch * lp, RWKV_DIM), BF16)
    if has_vres:
        in_specs += [const((LORA_COLS, RWKV_DIM)), chunk_spec]
        args += [wv, v_first]
        out_shape, out_specs = o_shape, chunk_spec
    else:
        out_shape = (o_shape, jax.ShapeDtypeStruct((batch * lp, RWKV_DIM), F32))
        out_specs = (chunk_spec, chunk_spec)
    return pl.pallas_call(
        functools.partial(_rwkv_kernel, has_vres=has_vres),
        out_shape=out_shape,
        grid=(batch, nc),
        in_specs=in_specs,
        out_specs=out_specs,
        scratch_shapes=[pltpu.VMEM((N_PAIRS, LANES, LANES), F32)],
        compiler_params=_cparams(("parallel", "arbitrary")),
        name="rwkv7_mix",
    )(*args)


def _merge_kernel(oa_ref, orw_ref, ga_ref, gr_ref, h_ref, pa_ref, pb_ref, wo_ref, nf_ref,
                  wrh_ref, wrl_ref, br_ref, hn_ref, xn_ref, route_ref, *, tiles_per_seq):
    i = pl.program_id(0)
    tm = h_ref.shape[0]
    ya = jnp.dot(oa_ref[...], pa_ref[...], preferred_element_type=F32)
    yb = jnp.dot(orw_ref[...], pb_ref[...], preferred_element_type=F32)
    mixed = ga_ref[...].astype(F32) * ya + gr_ref[...].astype(F32) * yb
    upd = jnp.dot(mixed.astype(BF16), wo_ref[...], preferred_element_type=F32)
    rows = lax.broadcasted_iota(jnp.int32, (tm, 1), 0) + (i % tiles_per_seq) * tm
    real = rows >= PAD
    h = jnp.where(real, h_ref[...] + upd, 0.0)
    hn_ref[...] = h
    xn = h * lax.rsqrt(jnp.mean(h * h, axis=-1, keepdims=True) + NORM_EPS) * nf_ref[...]
    xn_ref[...] = xn

    x_hi = xn.astype(BF16)
    x_lo = (xn - x_hi.astype(F32)).astype(BF16)
    logits = (jnp.dot(x_hi, wrh_ref[...], preferred_element_type=F32)
              + jnp.dot(x_hi, wrl_ref[...], preferred_element_type=F32)
              + jnp.dot(x_lo, wrh_ref[...], preferred_element_type=F32)) + br_ref[...]
    col = lax.broadcasted_iota(jnp.int32, logits.shape, 1).astype(F32)
    neg = -jnp.inf
    big = float(ROUTE_COLS)
    coarse = jnp.where(col < N_GROUPS, logits, neg)
    c_max = jnp.max(coarse, axis=-1, keepdims=True)
    grp = jnp.min(jnp.where(coarse == c_max, col, big), axis=-1, keepdims=True)
    p_grp = 1.0 / jnp.sum(jnp.exp(coarse - c_max), axis=-1, keepdims=True)
    lo_col = N_GROUPS + grp * EXPERTS_PER_GROUP
    fine = jnp.where((col >= lo_col) & (col < lo_col + EXPERTS_PER_GROUP), logits, neg)
    f1 = jnp.max(fine, axis=-1, keepdims=True)
    i1 = jnp.min(jnp.where(fine == f1, col, big), axis=-1, keepdims=True)
    fine2 = jnp.where(col == i1, neg, fine)
    f2 = jnp.max(fine2, axis=-1, keepdims=True)
    i2 = jnp.min(jnp.where(fine2 == f2, col, big), axis=-1, keepdims=True)
    e21 = jnp.exp(f2 - f1)
    w1 = p_grp / (1.0 + e21)
    w2 = p_grp * e21 / (1.0 + e21)
    e1 = jnp.where(real, i1 - N_GROUPS, N_EXPERTS).astype(F32)
    e2 = jnp.where(real, i2 - N_GROUPS, N_EXPERTS).astype(F32)
    w1 = jnp.where(real, w1, 0.0)
    w2 = jnp.where(real, w2, 0.0)
    route_ref[...] = jnp.where(col == 0, e1, jnp.where(col == 1, e2,
                               jnp.where(col == 2, w1, jnp.where(col == 3, w2, 0.0))))


def _merge(o_att, o_rwkv, gates, h, pa, pb, wo, nf, wr_hi, wr_lo, br, lp):
    t, d = h.shape
    tiles_per_seq = 8
    tm = lp // tiles_per_seq
    row = lambda w: pl.BlockSpec((tm, w), lambda i: (i, 0))
    const = lambda shape: pl.BlockSpec(shape, lambda i: (0,) * len(shape))
    return pl.pallas_call(
        functools.partial(_merge_kernel, tiles_per_seq=tiles_per_seq),
        out_shape=(jax.ShapeDtypeStruct((t, d), F32), jax.ShapeDtypeStruct((t, d), F32),
                   jax.ShapeDtypeStruct((t, ROUTE_COLS), F32)),
        grid=(t // tm,),
        in_specs=[row(Q_DIM), row(RWKV_DIM),
                  pl.BlockSpec((tm, d), lambda i: (i, 0)), pl.BlockSpec((tm, d), lambda i: (i, 1)),
                  row(d), const((Q_DIM, d)), const((RWKV_DIM, d)), const((d, d)), const((1, d)),
                  const((d, ROUTE_COLS)), const((d, ROUTE_COLS)), const((1, ROUTE_COLS))],
        out_specs=(row(d), row(d), row(ROUTE_COLS)),
        compiler_params=_cparams(("parallel",)),
        name="merge_route",
    )(o_att, o_rwkv, gates, gates, h, pa, pb, wo, nf.reshape(1, d), wr_hi, wr_lo, br)


def _moe_kernel(blk_expert_ref, row_tok_ref, nused_ref, x_hbm, wg_ref, wu_ref, wd_ref,
                y_ref, xbuf, wg_s, wu_s, wd_s, sem):
    i = pl.program_id(0)
    nused = nused_ref[0]
    slot = i % 2

    def gather(block, slot_):
        def body(r, carry):
            tok = row_tok_ref[block * MOE_BLOCK + r]
            pltpu.make_async_copy(x_hbm.at[pl.ds(tok, 1)], xbuf.at[slot_, pl.ds(r, 1)],
                                  sem.at[slot_]).start()
            return carry
        lax.fori_loop(0, MOE_BLOCK, body, 0)

    @pl.when((i == 0) & (nused > 0))
    def _():
        gather(0, 0)

    @pl.when(i + 1 < nused)
    def _():
        gather(i + 1, 1 - slot)

    new_expert = (i == 0) | (blk_expert_ref[i] != blk_expert_ref[jnp.maximum(i - 1, 0)])

    @pl.when((i < nused) & new_expert)
    def _():
        wg_s[...] = wg_ref[0].astype(BF16)
        wu_s[...] = wu_ref[0].astype(BF16)
        wd_s[...] = wd_ref[0].astype(BF16)

    @pl.when(i < nused)
    def _():
        pltpu.make_async_copy(x_hbm.at[pl.ds(0, MOE_BLOCK)], xbuf.at[slot], sem.at[slot]).wait()
        xb = xbuf[slot].astype(BF16)
        g = jnp.dot(xb, wg_s[...], preferred_element_type=F32)
        u = jnp.dot(xb, wu_s[...], preferred_element_type=F32)
        hid = (g * _sigmoid(g) * u).astype(BF16)
        y_ref[...] = jnp.dot(hid, wd_s[...], preferred_element_type=F32)

    @pl.when(i >= nused)
    def _():
        y_ref[...] = jnp.zeros_like(y_ref)


def _moe(blk_expert, row_tok, nused, xn, w_gate, w_up, w_down):
    n_blocks = blk_expert.shape[0]
    d = xn.shape[1]
    wspec = lambda shape: pl.BlockSpec((1,) + shape, lambda i, be, rt, nu: (be[i], 0, 0))
    return pl.pallas_call(
        _moe_kernel,
        out_shape=jax.ShapeDtypeStruct((n_blocks * MOE_BLOCK, d), F32),
        grid_spec=pltpu.PrefetchScalarGridSpec(
            num_scalar_prefetch=3,
            grid=(n_blocks,),
            in_specs=[pl.BlockSpec(memory_space=pl.ANY),
                      wspec((d, D_EXPERT)), wspec((d, D_EXPERT)), wspec((D_EXPERT, d))],
            out_specs=pl.BlockSpec((MOE_BLOCK, d), lambda i, be, rt, nu: (i, 0)),
            scratch_shapes=[pltpu.VMEM((2, MOE_BLOCK, d), F32),
                            pltpu.VMEM((d, D_EXPERT), BF16), pltpu.VMEM((d, D_EXPERT), BF16),
                            pltpu.VMEM((D_EXPERT, d), BF16),
                            pltpu.SemaphoreType.DMA((2,))]),
        compiler_params=_cparams(("arbitrary",)),
        name="moe_experts",
    )(blk_expert, row_tok, nused, xn, w_gate, w_up, w_down)


def _combine_kernel(pos_ref, y_hbm, h_ref, route_ref, o_ref, ybuf, sem, *, tiles_per_seq):
    i = pl.program_id(0)
    n = pl.num_programs(0)
    tm = h_ref.shape[0]
    slot = i % 2

    def gather(tile, slot_):
        def body(r, carry):
            for j in range(2):
                p = pos_ref[(tile * tm + r) * 2 + j]
                pltpu.make_async_copy(y_hbm.at[pl.ds(p, 1)], ybuf.at[slot_, j, pl.ds(r, 1)],
                                      sem.at[slot_]).start()
            return carry
        lax.fori_loop(0, tm, body, 0)

    @pl.when(i == 0)
    def _():
        gather(0, 0)

    @pl.when(i + 1 < n)
    def _():
        gather(i + 1, 1 - slot)

    for j in range(2):
        pltpu.make_async_copy(y_hbm.at[pl.ds(0, tm)], ybuf.at[slot, j], sem.at[slot]).wait()
    route = route_ref[...]
    rows = lax.broadcasted_iota(jnp.int32, (tm, 1), 0) + (i % tiles_per_seq) * tm
    out = h_ref[...] + route[:, 2:3] * ybuf[slot, 0] + route[:, 3:4] * ybuf[slot, 1]
    o_ref[...] = jnp.where(rows >= PAD, out, 0.0)


def _combine(pos, y_rows, h, route, lp):
    t, d = h.shape
    tm = BLOCK
    return pl.pallas_call(
        functools.partial(_combine_kernel, tiles_per_seq=lp // tm),
        out_shape=jax.ShapeDtypeStruct((t, d), F32),
        grid_spec=pltpu.PrefetchScalarGridSpec(
            num_scalar_prefetch=1,
            grid=(t // tm,),
            in_specs=[pl.BlockSpec(memory_space=pl.ANY),
                      pl.BlockSpec((tm, d), lambda i, pos: (i, 0)),
                      pl.BlockSpec((tm, ROUTE_COLS), lambda i, pos: (i, 0))],
            out_specs=pl.BlockSpec((tm, d), lambda i, pos: (i, 0)),
            scratch_shapes=[pltpu.VMEM((2, 2, tm, d), F32), pltpu.SemaphoreType.DMA((2,))]),
        compiler_params=_cparams(("arbitrary",)),
        name="moe_combine",
    )(pos, y_rows, h, route)


def _dispatch(route, n_real_tokens):
    t = route.shape[0]
    e_flat = route[:, :2].astype(jnp.int32).reshape(t * 2)
    n_assign = n_real_tokens * 2
    n_blocks = -(-(n_assign + N_EXPERTS * (MOE_BLOCK - 1)) // MOE_BLOCK)
    onehot = (e_flat[:, None] == jnp.arange(N_EXPERTS, dtype=jnp.int32)[None, :]).astype(jnp.int32)
    ranks = jnp.cumsum(onehot, axis=0) - onehot
    counts = jnp.sum(onehot, axis=0)
    padded = ((counts + MOE_BLOCK - 1) // MOE_BLOCK) * MOE_BLOCK
    pend = jnp.cumsum(padded)
    pstart = pend - padded
    valid = e_flat < N_EXPERTS
    dest = jnp.sum(onehot * (ranks + pstart[None, :]), axis=1)
    p_rows = n_blocks * MOE_BLOCK
    tok = jnp.arange(t * 2, dtype=jnp.int32) // 2
    row_tok = jnp.zeros((p_rows,), jnp.int32).at[jnp.where(valid, dest, p_rows)].set(tok, mode="drop")
    blk_start = jnp.arange(n_blocks, dtype=jnp.int32) * MOE_BLOCK
    blk_expert = jnp.minimum(jnp.sum(blk_start[:, None] >= pend[None, :], axis=1), N_EXPERTS - 1)
    nused = (pend[-1] // MOE_BLOCK).astype(jnp.int32).reshape(1)
    pos = jnp.where(valid, dest, 0).astype(jnp.int32)
    return blk_expert.astype(jnp.int32), row_tok, nused, pos


def _pad_rows(w, rows, offset):
    out = jnp.zeros((rows, w.shape[1]), w.dtype)
    return out.at[offset:offset + w.shape[0]].set(w)


def kernel(x, meta_tokens, norm_mix, w_in, b_qkv, sinks, rwkv_mu, rwkv_w0, rwkv_w_decay_up, rwkv_a0, rwkv_w_iclr_up, rwkv_w_gate_up, rwkv_k_k, rwkv_k_a, rwkv_r_k, rwkv_ln_w, rwkv_ln_b, rwkv_vres_down, rwkv_vres_mu, rwkv_v0, rwkv_vres_up, w_branch_att, w_branch_rwkv, w_out, norm_ffn, w_coarse, b_coarse, w_fine, b_fine, w_exp_gate, w_exp_up, w_exp_down, norm_final):
    batch, seq, d = x.shape
    depth = w_in.shape[0]
    lp = PAD + N_META + seq
    t = batch * lp
    assert d == D_MODEL and lp % (8 * 16) == 0

    meta = jnp.broadcast_to(meta_tokens.astype(x.dtype)[None], (batch, N_META, d))
    h = jnp.concatenate([jnp.zeros((batch, PAD, d), x.dtype), meta, x], axis=1).reshape(t, d)
    tables = _rope_tables(lp)
    tm_proj = lp // 2
    tm_norm = lp // 8
    o_gate = QKV_DIM + GATE_DIM
    v_first = None
    for l in range(depth):
        u = _rmsnorm(h, norm_mix[l], tm_norm, BF16)
        w_l = w_in[l]
        w_qkv = w_l[:, :QKV_DIM].astype(BF16)
        w_gate = w_l[:, QKV_DIM:o_gate].astype(BF16)
        w_rkv = w_l[:, o_gate:o_gate + RKV_COLS]
        w_lora = w_l[:, o_gate + RKV_COLS:]
        mu = rwkv_mu[l]
        lora_parts = [w_lora]
        mu_parts = [mu]
        if l > 0:
            lora_parts.append(rwkv_vres_down[l - 1])
            mu_parts.append(rwkv_vres_mu[l - 1])
        n_lora = sum(p.shape[1] for p in lora_parts)
        lora_parts.append(jnp.zeros((d, LORA_COLS - n_lora), F32))
        mu_parts.append(jnp.zeros((LORA_COLS - n_lora,), F32))
        w_rwkv = jnp.concatenate([w_rkv] + lora_parts, axis=1).astype(BF16)
        mu_full = jnp.concatenate(mu_parts).reshape(1, RWKV_COLS)

        zqkv = _proj(u, w_qkv, b_qkv[l], tm_proj, None, F32, "proj_qkv")
        gates = _proj(u, w_gate, jnp.zeros((GATE_DIM,), F32), tm_proj, "sigmoid", BF16, "proj_gate")
        z_r = _proj(u, w_rwkv, jnp.zeros((RWKV_COLS,), F32), tm_proj, None, F32, "proj_rwkv")

        o_att = _attention(zqkv, sinks[l], tables, batch, lp)

        vecs = jnp.stack([rwkv_w0[l], rwkv_a0[l], rwkv_k_k[l], rwkv_k_a[l], rwkv_r_k[l].reshape(-1),
                          rwkv_ln_w[l], rwkv_ln_b[l],
                          rwkv_v0[l - 1] if l > 0 else jnp.zeros((RWKV_DIM,), F32)])
        wd = _pad_rows(rwkv_w_decay_up[l], LORA_COLS, 0).astype(BF16)
        wa = _pad_rows(rwkv_w_iclr_up[l], LORA_COLS, LORA).astype(BF16)
        wg = _pad_rows(rwkv_w_gate_up[l], LORA_COLS, 2 * LORA).astype(BF16)
        if l == 0:
            o_rwkv, v_first = _rwkv(z_r, mu_full, vecs, wd, wa, wg, None, None, batch, lp)
        else:
            wv = _pad_rows(rwkv_vres_up[l - 1], LORA_COLS, 3 * LORA).astype(BF16)
            o_rwkv = _rwkv(z_r, mu_full, vecs, wd, wa, wg, wv, v_first, batch, lp)

        w_route = jnp.concatenate(
            [w_coarse[l], w_fine[l].reshape(d, N_EXPERTS),
             jnp.zeros((d, ROUTE_COLS - N_GROUPS - N_EXPERTS), F32)], axis=1)
        b_route = jnp.concatenate(
            [b_coarse[l], b_fine[l].reshape(N_EXPERTS),
             jnp.zeros((ROUTE_COLS - N_GROUPS - N_EXPERTS,), F32)]).reshape(1, ROUTE_COLS)
        wr_hi = w_route.astype(BF16)
        wr_lo = (w_route - wr_hi.astype(F32)).astype(BF16)
        h, xn, route = _merge(o_att, o_rwkv, gates, h, w_branch_att[l].astype(BF16),
                              w_branch_rwkv[l].astype(BF16), w_out[l].astype(BF16), norm_ffn[l],
                              wr_hi, wr_lo, b_route, lp)

        blk_expert, row_tok, nused, pos = _dispatch(route, batch * (N_META + seq))
        y_rows = _moe(blk_expert, row_tok, nused, xn, w_exp_gate[l], w_exp_up[l], w_exp_down[l])
        h = _combine(pos, y_rows, h, route, lp)

    out = _rmsnorm(h, norm_final, tm_norm, x.dtype)
    return out.reshape(batch, lp, d)[:, PAD + N_META:]
```

```python
import functools

import jax
import jax.numpy as jnp
from jax import lax
from jax.experimental import pallas as pl
from jax.experimental.pallas import tpu as pltpu

F32 = jnp.float32
BF16 = jnp.bfloat16

D_MODEL = 2048
N_META = 16
NORM_EPS = 1e-5
HEAD_DIM = 64
ATT_HEADS = 16
ATT_KV_HEADS = 4
ATT_GROUP = ATT_HEADS // ATT_KV_HEADS
Q_DIM = ATT_HEADS * HEAD_DIM
KV_DIM = ATT_KV_HEADS * HEAD_DIM
QKV_DIM = Q_DIM + 2 * KV_DIM
BLOCK = 128
ROPE_DIM = HEAD_DIM // 4
ROPE_HALF = ROPE_DIM // 2
ROPE_THETA = 500000.0
RWKV_DIM = 1024
RWKV_HEADS = 16
LORA = 64
VRES_LORA = 32
GN_EPS = 64e-5
GATE_DIM = 2 * D_MODEL
N_GROUPS = 8
EXPERTS_PER_GROUP = 8
N_EXPERTS = N_GROUPS * EXPERTS_PER_GROUP
D_EXPERT = D_MODEL // 4
MOE_BLOCK = 128

PAD = BLOCK - N_META
LANES = 128
CHUNK = 64
N_PAIRS = RWKV_HEADS // 2
RKV_COLS = 3 * RWKV_DIM
LORA_COLS = 256
RWKV_COLS = RKV_COLS + LORA_COLS
ROUTE_COLS = 128
VMEM_LIMIT = 56 * 1024 * 1024


def _cparams(sem):
    return pltpu.CompilerParams(dimension_semantics=sem, vmem_limit_bytes=VMEM_LIMIT)


def _dot(a, b):
    return jnp.dot(a.astype(BF16), b.astype(BF16), preferred_element_type=F32)


def _dot_nt(a, b):
    return lax.dot_general(a.astype(BF16), b.astype(BF16), (((1,), (1,)), ((), ())),
                           preferred_element_type=F32)


def _sigmoid(x):
    return 1.0 / (1.0 + jnp.exp(-x))


def _rmsnorm_kernel(h_ref, g_ref, o_ref):
    x = h_ref[...]
    y = x * lax.rsqrt(jnp.mean(x * x, axis=-1, keepdims=True) + NORM_EPS)
    o_ref[...] = (y * g_ref[...]).astype(o_ref.dtype)


def _rmsnorm(h, g, tm, out_dtype):
    t, d = h.shape
    return pl.pallas_call(
        _rmsnorm_kernel,
        out_shape=jax.ShapeDtypeStruct((t, d), out_dtype),
        grid=(t // tm,),
        in_specs=[pl.BlockSpec((tm, d), lambda i: (i, 0)),
                  pl.BlockSpec((1, d), lambda i: (0, 0))],
        out_specs=pl.BlockSpec((tm, d), lambda i: (i, 0)),
        compiler_params=_cparams(("parallel",)),
        name="rmsnorm",
    )(h, g.reshape(1, d))


def _proj_kernel(u_ref, w_ref, b_ref, o_ref, *, act):
    z = jnp.dot(u_ref[...], w_ref[...], preferred_element_type=F32) + b_ref[...]
    if act == "sigmoid":
        z = _sigmoid(z)
    o_ref[...] = z.astype(o_ref.dtype)


def _proj(u, w, b, tm, act, out_dtype, name):
    t, d = u.shape
    n = w.shape[1]
    tn = 512 if n % 512 == 0 else 256
    return pl.pallas_call(
        functools.partial(_proj_kernel, act=act),
        out_shape=jax.ShapeDtypeStruct((t, n), out_dtype),
        grid=(t // tm, n // tn),
        in_specs=[pl.BlockSpec((tm, d), lambda i, j: (i, 0)),
                  pl.BlockSpec((d, tn), lambda i, j: (0, j)),
                  pl.BlockSpec((1, tn), lambda i, j: (0, j))],
        out_specs=pl.BlockSpec((tm, tn), lambda i, j: (i, j)),
        compiler_params=_cparams(("parallel", "arbitrary")),
        name=name,
    )(u, w, b.reshape(1, n))


def _rope(x, tab_ref):
    cos, sin_lo, sin_hi = tab_ref[0], tab_ref[1], tab_ref[2]
    outs = []
    for c in range(x.shape[1] // LANES):
        xc = x[:, c * LANES:(c + 1) * LANES]
        up = pltpu.roll(xc, LANES - ROPE_HALF, axis=1)
        dn = pltpu.roll(xc, ROPE_HALF, axis=1)
        outs.append(xc * cos + up * sin_lo + dn * sin_hi)
    return jnp.concatenate(outs, axis=1)


def _attn_kernel(sink_ref, q_ref, kc_ref, kp_ref, km_ref, vc_ref, vp_ref, vm_ref,
                 tc_ref, tp_ref, tm_ref, o_ref):
    n = pl.program_id(1)
    q = _rope(q_ref[...], tc_ref) * (HEAD_DIM ** -0.5)
    keys = jnp.concatenate([_rope(kp_ref[...], tp_ref), _rope(kc_ref[...], tc_ref),
                            _rope(km_ref[...], tm_ref)], axis=0).astype(BF16)
    vals = jnp.concatenate([vp_ref[...], vc_ref[...], vm_ref[...]], axis=0).astype(BF16)
    q = q.astype(BF16)

    rows = ATT_GROUP * BLOCK
    r = lax.broadcasted_iota(jnp.int32, (rows, 3 * BLOCK), 0) & (BLOCK - 1)
    c = lax.broadcasted_iota(jnp.int32, (rows, 3 * BLOCK), 1)
    band = (c > r) & (c <= r + BLOCK) & (c + BLOCK * n >= 2 * BLOCK)
    m = c - 2 * BLOCK
    meta = (m >= PAD) & (m <= r + n * BLOCK)
    ok = ((c < 2 * BLOCK) & band) | ((c >= 2 * BLOCK) & meta)
    g_of_row = lax.broadcasted_iota(jnp.int32, (rows, 1), 0) >> (BLOCK.bit_length() - 1)

    outs = []
    for h in range(ATT_KV_HEADS):
        qg = jnp.concatenate(
            [q[:, (h * ATT_GROUP + g) * HEAD_DIM:(h * ATT_GROUP + g + 1) * HEAD_DIM]
             for g in range(ATT_GROUP)], axis=0)
        kh = keys[:, h * HEAD_DIM:(h + 1) * HEAD_DIM]
        vh = vals[:, h * HEAD_DIM:(h + 1) * HEAD_DIM]
        s = lax.dot_general(qg, kh, (((1,), (1,)), ((), ())), preferred_element_type=F32)
        s = jnp.where(ok, s, -1e30)
        sink = jnp.zeros((rows, 1), F32)
        for g in range(ATT_GROUP):
            sink = jnp.where(g_of_row == g, sink_ref[h * ATT_GROUP + g], sink)
        mx = jnp.maximum(jnp.max(s, axis=-1, keepdims=True), sink)
        e = jnp.exp(s - mx)
        den = jnp.sum(e, axis=-1, keepdims=True) + jnp.exp(sink - mx)
        p = (e / den).astype(BF16)
        og = jnp.dot(p, vh, preferred_element_type=F32)
        outs.extend(og[g * BLOCK:(g + 1) * BLOCK] for g in range(ATT_GROUP))
    o_ref[...] = jnp.concatenate(outs, axis=1).astype(o_ref.dtype)


def _rope_tables(lp):
    pos = (jnp.arange(lp) - PAD).astype(F32)
    inv = jnp.power(jnp.float32(ROPE_THETA), -jnp.arange(ROPE_HALF, dtype=F32) / ROPE_HALF)
    ang = pos[:, None] * inv[None, :]
    cos, sin = jnp.cos(ang), jnp.sin(ang)
    zeros = jnp.zeros((lp, HEAD_DIM - ROPE_DIM), F32)
    zh = jnp.zeros((lp, ROPE_HALF), F32)
    c_tab = jnp.concatenate([cos, cos, jnp.ones_like(zeros)], axis=1)
    lo_tab = jnp.concatenate([-sin, zh, zeros], axis=1)
    hi_tab = jnp.concatenate([zh, sin, zeros], axis=1)
    tab = jnp.stack([c_tab, lo_tab, hi_tab])
    return jnp.tile(tab, (1, 1, LANES // HEAD_DIM))


def _attention(zqkv, sinks, tables, batch, lp):
    nb = lp // BLOCK
    qcols = Q_DIM // KV_DIM
    cur = lambda b, n: b * nb + n
    prev = lambda b, n: b * nb + jnp.maximum(n - 1, 0)
    first = lambda b, n: b * nb
    kv_spec = lambda rowf, col: pl.BlockSpec((BLOCK, KV_DIM), lambda b, n: (rowf(b, n), col))
    tab_spec = lambda f: pl.BlockSpec((3, BLOCK, LANES), lambda b, n: (0, f(b, n), 0))
    return pl.pallas_call(
        _attn_kernel,
        out_shape=jax.ShapeDtypeStruct((batch * lp, Q_DIM), BF16),
        grid=(batch, nb),
        in_specs=[pl.BlockSpec(memory_space=pltpu.SMEM),
                  pl.BlockSpec((BLOCK, Q_DIM), lambda b, n: (cur(b, n), 0)),
                  kv_spec(cur, qcols), kv_spec(prev, qcols), kv_spec(first, qcols),
                  kv_spec(cur, qcols + 1), kv_spec(prev, qcols + 1), kv_spec(first, qcols + 1),
                  tab_spec(lambda b, n: n), tab_spec(lambda b, n: jnp.maximum(n - 1, 0)),
                  tab_spec(lambda b, n: 0)],
        out_specs=pl.BlockSpec((BLOCK, Q_DIM), lambda b, n: (cur(b, n), 0)),
        compiler_params=_cparams(("parallel", "arbitrary")),
        name="swa_attention",
    )(sinks, zqkv, zqkv, zqkv, zqkv, zqkv, zqkv, zqkv, tables, tables, tables)


def _block_diag(p):
    lane = lax.broadcasted_iota(jnp.int32, p.shape, 1)
    return jnp.concatenate([jnp.where(lane < HEAD_DIM, p, 0.0),
                            jnp.where(lane >= HEAD_DIM, p, 0.0)], axis=0)


def _head_sum(x, ones_bd):
    hi = x.astype(BF16)
    lo = (x - hi.astype(F32)).astype(BF16)
    return (jnp.dot(hi, ones_bd, preferred_element_type=F32)
            + jnp.dot(lo, ones_bd, preferred_element_type=F32))


def _softplus(x):
    return jnp.maximum(x, 0.0) + jnp.log(1.0 + jnp.exp(-jnp.abs(x)))


def _rwkv_kernel(*refs, has_vres):
    if has_vres:
        (z_ref, zprev_ref, mu_ref, vec_ref, wd_ref, wa_ref, wg_ref, wv_ref, vfirst_ref,
         o_ref, h_scr) = refs
    else:
        (z_ref, zprev_ref, mu_ref, vec_ref, wd_ref, wa_ref, wg_ref,
         o_ref, vfirst_out_ref, h_scr) = refs
    c = pl.program_id(1)

    @pl.when(c == 0)
    def _():
        h_scr[...] = jnp.zeros_like(h_scr)

    z = z_ref[...]
    row = lax.broadcasted_iota(jnp.int32, z.shape, 0)
    prev_row = zprev_ref[7:8, :] * jnp.where(c > 0, 1.0, 0.0)
    z_shift = jnp.where(row == 0, prev_row, pltpu.roll(z, 1, axis=0))
    zs = z + (z_shift - z) * mu_ref[...]

    w0, a0, k_k, k_a, r_k, ln_w, ln_b, v0 = (vec_ref[i:i + 1, :] for i in range(8))
    r = zs[:, :RWKV_DIM]
    k = zs[:, RWKV_DIM:2 * RWKV_DIM]
    v = zs[:, 2 * RWKV_DIM:RKV_COLS]
    lora = zs[:, RKV_COLS:]
    w_log = -_softplus(-(w0 + _dot(jnp.tanh(lora), wd_ref[...]))) - 0.5
    lw = -jnp.exp(w_log)
    a_sig = _sigmoid(a0 + _dot(lora, wa_ref[...]))
    gate = _dot(_sigmoid(lora), wg_ref[...])
    if has_vres:
        v = v + (vfirst_ref[...] - v) * _sigmoid(v0 + _dot(lora, wv_ref[...]))
    else:
        vfirst_out_ref[...] = v

    lane2 = lax.broadcasted_iota(jnp.int32, (LANES, LANES), 1) >= HEAD_DIM
    row2 = lax.broadcasted_iota(jnp.int32, (LANES, LANES), 0) >= HEAD_DIM
    same_head = lane2 == row2
    ones_bd = same_head.astype(BF16)

    def per_pair(fn, *xs):
        return jnp.concatenate(
            [fn(*(x[:, p * LANES:(p + 1) * LANES] for x in xs)) for p in range(N_PAIRS)], axis=1)

    kk = k * k_k
    ssq = per_pair(lambda x: _head_sum(x * x, ones_bd), kk)
    kk = kk / jnp.maximum(jnp.sqrt(ssq), 1e-12)
    k = k * (1.0 + (a_sig - 1.0) * k_a)
    a_vec = -kk
    b_vec = kk * a_sig

    trow = lax.broadcasted_iota(jnp.int32, lw.shape, 0)
    cum = lw
    sh = 1
    while sh < CHUNK:
        cum = cum + jnp.where(trow >= sh, pltpu.roll(cum, sh, axis=0), 0.0)
        sh *= 2
    tot = cum[CHUNK - 1:CHUNK, :]
    e_pos = jnp.exp(cum)
    e_neg = jnp.exp(-cum)
    e_tot = jnp.exp(tot)
    a_t = a_vec * jnp.exp(cum - lw)
    r_t = r * e_pos
    b_t = b_vec * e_neg
    k_t = k * e_neg
    b_h = b_t * e_tot
    k_h = k_t * e_tot

    t_idx = lax.broadcasted_iota(jnp.int32, (CHUNK, LANES), 0)
    s_idx = lax.broadcasted_iota(jnp.int32, (CHUNK, LANES), 1) & (HEAD_DIM - 1)
    strict = t_idx > s_idx
    incl = t_idx >= s_idx
    eye = (t_idx == s_idx).astype(F32)
    diag_mask = (lax.broadcasted_iota(jnp.int32, (LANES, LANES), 0)
                 == lax.broadcasted_iota(jnp.int32, (LANES, LANES), 1))

    sl = lambda x, p: x[:, p * LANES:(p + 1) * LANES]
    pairs = range(N_PAIRS)
    at = [sl(a_t, p) for p in pairs]
    rt = [sl(r_t, p) for p in pairs]
    vp = [sl(v, p) for p in pairs]

    a_ab, a_ak, m_rb, m_rk = [], [], [], []
    for p in pairs:
        lhs = jnp.concatenate([at[p], rt[p]], axis=0)
        rhs = jnp.concatenate([_block_diag(sl(b_t, p)), _block_diag(sl(k_t, p))], axis=0)
        sc = _dot_nt(lhs, rhs)
        a_ab.append(jnp.where(strict, sc[:CHUNK, :LANES], 0.0))
        a_ak.append(jnp.where(strict, sc[:CHUNK, LANES:], 0.0))
        m_rb.append(jnp.where(incl, sc[CHUNK:, :LANES], 0.0))
        m_rk.append(jnp.where(incl, sc[CHUNK:, LANES:], 0.0))

    pw = [_dot(a_ab[p], _block_diag(a_ab[p])) for p in pairs]
    tm = [eye + a_ab[p] for p in pairs]
    n_stage = CHUNK.bit_length() - 2
    for stage in range(n_stage):
        last = stage == n_stage - 1
        for p in pairs:
            if last:
                tm[p] = tm[p] + _dot(pw[p], _block_diag(tm[p]))
            else:
                res = _dot(pw[p], jnp.concatenate([_block_diag(pw[p]), _block_diag(tm[p])], axis=1))
                pw[p] = res[:, :LANES]
                tm[p] = tm[p] + res[:, LANES:]

    outs = []
    for p in pairs:
        x = _dot(a_ak[p], _block_diag(vp[p]))
        wu = _dot(tm[p], jnp.concatenate([_block_diag(at[p]), _block_diag(x)], axis=1))
        w_m, u0 = wu[:, :LANES], wu[:, LANES:]
        qo = _dot(m_rb[p], jnp.concatenate([_block_diag(w_m), _block_diag(u0)], axis=1))
        q_hat = rt[p] + qo[:, :LANES]
        o0 = qo[:, LANES:] + _dot(m_rk[p], _block_diag(vp[p]))
        gh = _dot(sl(b_h, p).T, wu)
        kv = _dot(sl(k_h, p).T, vp[p])
        g_m = jnp.where(same_head, gh[:, :LANES], 0.0) + jnp.where(diag_mask, sl(e_tot, p), 0.0)
        h_add = jnp.where(same_head, gh[:, LANES:] + kv, 0.0)
        ser = _dot(jnp.concatenate([q_hat, g_m], axis=0), h_scr[p])
        outs.append(ser[:CHUNK] + o0)
        h_scr[p] = ser[CHUNK:] + h_add
    y = jnp.concatenate(outs, axis=1)

    inv_n = 1.0 / HEAD_DIM
    mean = per_pair(lambda x: _head_sum(x, ones_bd), y) * inv_n
    yc = y - mean
    var = per_pair(lambda x: _head_sum(x * x, ones_bd), yc) * inv_n
    yn = yc * lax.rsqrt(var + GN_EPS) * ln_w + ln_b
    bonus = per_pair(lambda x: _head_sum(x, ones_bd), r * k * r_k)
    o_ref[...] = ((yn + bonus * v) * gate).astype(o_ref.dtype)


def _rwkv(z_r, mu, vecs, wd, wa, wg, wv, v_first, batch, lp):
    nc = lp // CHUNK
    has_vres = v_first is not None
    row_blk = lambda b, c: b * nc + c
    const = lambda shape: pl.BlockSpec(shape, lambda b, c: (0,) * len(shape))
    in_specs = [pl.BlockSpec((CHUNK, RWKV_COLS), lambda b, c: (row_blk(b, c), 0)),
                pl.BlockSpec((8, RWKV_COLS),
                             lambda b, c: (jnp.maximum(row_blk(b, c) * (CHUNK // 8) - 1, 0), 0)),
                const((1, RWKV_COLS)), const((8, RWKV_DIM)),
                const((LORA_COLS, RWKV_DIM)), const((LORA_COLS, RWKV_DIM)),
                const((LORA_COLS, RWKV_DIM))]
    args = [z_r, z_r, mu, vecs, wd, wa, wg]
    chunk_spec = pl.BlockSpec((CHUNK, RWKV_DIM), lambda b, c: (row_blk(b, c), 0))
    o_shape = jax.ShapeDtypeStruct((batch * lp, RWKV_DIM), BF16)
    if has_vres:
        in_specs += [const((LORA_COLS, RWKV_DIM)), chunk_spec]
        args += [wv, v_first]
        out_shape, out_specs = o_shape, chunk_spec
    else:
        out_shape = (o_shape, jax.ShapeDtypeStruct((batch * lp, RWKV_DIM), F32))
        out_specs = (chunk_spec, chunk_spec)
    return pl.pallas_call(
        functools.partial(_rwkv_kernel, has_vres=has_vres),
        out_shape=out_shape,
        grid=(batch, nc),
        in_specs=in_specs,
        out_specs=out_specs,
        scratch_shapes=[pltpu.VMEM((N_PAIRS, LANES, LANES), F32)],
        compiler_params=_cparams(("parallel", "arbitrary")),
        name="rwkv7_mix",
    )(*args)


def _merge_kernel(oa_ref, orw_ref, ga_ref, gr_ref, h_ref, pa_ref, pb_ref, wo_ref, nf_ref,
                  wrh_ref, wrl_ref, br_ref, hn_ref, xn_ref, route_ref, *, tiles_per_seq):
    i = pl.program_id(0)
    tm = h_ref.shape[0]
    ya = jnp.dot(oa_ref[...], pa_ref[...], preferred_element_type=F32)
    yb = jnp.dot(orw_ref[...], pb_ref[...], preferred_element_type=F32)
    mixed = ga_ref[...].astype(F32) * ya + gr_ref[...].astype(F32) * yb
    upd = jnp.dot(mixed.astype(BF16), wo_ref[...], preferred_element_type=F32)
    rows = lax.broadcasted_iota(jnp.int32, (tm, 1), 0) + (i % tiles_per_seq) * tm
    real = rows >= PAD
    h = jnp.where(real, h_ref[...] + upd, 0.0)
    hn_ref[...] = h
    xn = h * lax.rsqrt(jnp.mean(h * h, axis=-1, keepdims=True) + NORM_EPS) * nf_ref[...]
    xn_ref[...] = xn

    x_hi = xn.astype(BF16)
    x_lo = (xn - x_hi.astype(F32)).astype(BF16)
    logits = (jnp.dot(x_hi, wrh_ref[...], preferred_element_type=F32)
              + jnp.dot(x_hi, wrl_ref[...], preferred_element_type=F32)
              + jnp.dot(x_lo, wrh_ref[...], preferred_element_type=F32)) + br_ref[...]
    col = lax.broadcasted_iota(jnp.int32, logits.shape, 1).astype(F32)
    neg = -jnp.inf
    big = float(ROUTE_COLS)
    coarse = jnp.where(col < N_GROUPS, logits, neg)
    c_max = jnp.max(coarse, axis=-1, keepdims=True)
    grp = jnp.min(jnp.where(coarse == c_max, col, big), axis=-1, keepdims=True)
    p_grp = 1.0 / jnp.sum(jnp.exp(coarse - c_max), axis=-1, keepdims=True)
    lo_col = N_GROUPS + grp * EXPERTS_PER_GROUP
    fine = jnp.where((col >= lo_col) & (col < lo_col + EXPERTS_PER_GROUP), logits, neg)
    f1 = jnp.max(fine, axis=-1, keepdims=True)
    i1 = jnp.min(jnp.where(fine == f1, col, big), axis=-1, keepdims=True)
    fine2 = jnp.where(col == i1, neg, fine)
    f2 = jnp.max(fine2, axis=-1, keepdims=True)
    i2 = jnp.min(jnp.where(fine2 == f2, col, big), axis=-1, keepdims=True)
    e21 = jnp.exp(f2 - f1)
    w1 = p_grp / (1.0 + e21)
    w2 = p_grp * e21 / (1.0 + e21)
    e1 = jnp.where(real, i1 - N_GROUPS, N_EXPERTS).astype(F32)
    e2 = jnp.where(real, i2 - N_GROUPS, N_EXPERTS).astype(F32)
    w1 = jnp.where(real, w1, 0.0)
    w2 = jnp.where(real, w2, 0.0)
    route_ref[...] = jnp.where(col == 0, e1, jnp.where(col == 1, e2,
                               jnp.where(col == 2, w1, jnp.where(col == 3, w2, 0.0))))


def _merge(o_att, o_rwkv, gates, h, pa, pb, wo, nf, wr_hi, wr_lo, br, lp):
    t, d = h.shape
    tiles_per_seq = 8
    tm = lp // tiles_per_seq
    row = lambda w: pl.BlockSpec((tm, w), lambda i: (i, 0))
    const = lambda shape: pl.BlockSpec(shape, lambda i: (0,) * len(shape))
    return pl.pallas_call(
        functools.partial(_merge_kernel, tiles_per_seq=tiles_per_seq),
        out_shape=(jax.ShapeDtypeStruct((t, d), F32), jax.ShapeDtypeStruct((t, d), F32),
                   jax.ShapeDtypeStruct((t, ROUTE_COLS), F32)),
        grid=(t // tm,),
        in_specs=[row(Q_DIM), row(RWKV_DIM),
                  pl.BlockSpec((tm, d), lambda i: (i, 0)), pl.BlockSpec((tm, d), lambda i: (i, 1)),
                  row(d), const((Q_DIM, d)), const((RWKV_DIM, d)), const((d, d)), const((1, d)),
                  const((d, ROUTE_COLS)), const((d, ROUTE_COLS)), const((1, ROUTE_COLS))],
        out_specs=(row(d), row(d), row(ROUTE_COLS)),
        compiler_params=_cparams(("parallel",)),
        name="merge_route",
    )(o_att, o_rwkv, gates, gates, h, pa, pb, wo, nf.reshape(1, d), wr_hi, wr_lo, br)


def _moe_kernel(blk_expert_ref, first_ref, next_ref, row_tok_ref, nused_ref, x_hbm,
                wg_hbm, wu_hbm, wd_hbm, y_ref, xbuf, wg_f, wu_f, wd_f, wg_s, wu_s, wd_s,
                xsem, wsem, *, layer):
    i = pl.program_id(0)
    n_blocks = pl.num_programs(0)
    nused = nused_ref[0]
    slot = i % 2

    def weight_copies(e):
        return (pltpu.make_async_copy(wg_hbm.at[layer, e], wg_f, wsem.at[0]),
                pltpu.make_async_copy(wu_hbm.at[layer, e], wu_f, wsem.at[1]),
                pltpu.make_async_copy(wd_hbm.at[layer, e], wd_f, wsem.at[2]))

    def gather(block, slot_):
        for r in range(MOE_BLOCK):
            tok = row_tok_ref[block * MOE_BLOCK + r]
            pltpu.make_async_copy(x_hbm.at[pl.ds(tok, 1)], xbuf.at[slot_, pl.ds(r, 1)],
                                  xsem.at[slot_]).start(priority=r % 2)

    def wait_rows(slot_):
        pltpu.make_async_copy(x_hbm.at[pl.ds(0, MOE_BLOCK)], xbuf.at[slot_], xsem.at[slot_]).wait()

    @pl.when((i == 0) & (nused > 0))
    def _():
        for cp in weight_copies(blk_expert_ref[0]):
            cp.start()
        gather(0, 0)

    @pl.when((i < nused) & (first_ref[i] == 1))
    def _():
        for cp in weight_copies(blk_expert_ref[i]):
            cp.wait()
        wg_s[...] = wg_f[...].astype(BF16)
        wu_s[...] = wu_f[...].astype(BF16)
        wd_s[...] = wd_f[...].astype(BF16)

        @pl.when(next_ref[i] >= 0)
        def _():
            for cp in weight_copies(next_ref[i]):
                cp.start()

    @pl.when(i < nused)
    def _():
        wait_rows(slot)
        gather(jnp.minimum(i + 1, n_blocks - 1), 1 - slot)
        xb = xbuf[slot].astype(BF16)
        g = jnp.dot(xb, wg_s[...], preferred_element_type=F32)
        u = jnp.dot(xb, wu_s[...], preferred_element_type=F32)
        hid = (g * _sigmoid(g) * u).astype(BF16)
        y_ref[...] = jnp.dot(hid, wd_s[...], preferred_element_type=F32)

        @pl.when(i == n_blocks - 1)
        def _():
            wait_rows(1 - slot)

    @pl.when(i >= nused)
    def _():
        y_ref[...] = jnp.zeros_like(y_ref)

        @pl.when((i == nused) & (nused > 0))
        def _():
            wait_rows(slot)


def _moe(dispatch, xn, w_gate, w_up, w_down, layer):
    blk_expert, first, nxt, row_tok, nused = dispatch
    n_blocks = blk_expert.shape[0]
    d = xn.shape[1]
    any_spec = pl.BlockSpec(memory_space=pl.ANY)
    return pl.pallas_call(
        functools.partial(_moe_kernel, layer=layer),
        out_shape=jax.ShapeDtypeStruct((n_blocks * MOE_BLOCK, d), F32),
        grid_spec=pltpu.PrefetchScalarGridSpec(
            num_scalar_prefetch=5,
            grid=(n_blocks,),
            in_specs=[any_spec, any_spec, any_spec, any_spec],
            out_specs=pl.BlockSpec((MOE_BLOCK, d), lambda i, *_: (i, 0)),
            scratch_shapes=[pltpu.VMEM((2, MOE_BLOCK, d), F32),
                            pltpu.VMEM((d, D_EXPERT), F32), pltpu.VMEM((d, D_EXPERT), F32),
                            pltpu.VMEM((D_EXPERT, d), F32),
                            pltpu.VMEM((d, D_EXPERT), BF16), pltpu.VMEM((d, D_EXPERT), BF16),
                            pltpu.VMEM((D_EXPERT, d), BF16),
                            pltpu.SemaphoreType.DMA((2,)), pltpu.SemaphoreType.DMA((3,))]),
        compiler_params=_cparams(("arbitrary",)),
        name="moe_experts",
    )(blk_expert, first, nxt, row_tok, nused, xn, w_gate, w_up, w_down)


def _combine_kernel(pos_ref, y_hbm, h_ref, route_ref, o_ref, ybuf, sem, *, tiles_per_seq):
    i = pl.program_id(0)
    n = pl.num_programs(0)
    tm = h_ref.shape[0]
    slot = i % 2

    def gather(tile, slot_):
        for r in range(tm):
            for j in range(2):
                p = pos_ref[(tile * tm + r) * 2 + j]
                pltpu.make_async_copy(y_hbm.at[pl.ds(p, 1)], ybuf.at[slot_, j, pl.ds(r, 1)],
                                      sem.at[slot_]).start(priority=j)

    @pl.when(i == 0)
    def _():
        gather(0, 0)

    @pl.when(i + 1 < n)
    def _():
        gather(i + 1, 1 - slot)

    for j in range(2):
        pltpu.make_async_copy(y_hbm.at[pl.ds(0, tm)], ybuf.at[slot, j], sem.at[slot]).wait()
    route = route_ref[...]
    rows = lax.broadcasted_iota(jnp.int32, (tm, 1), 0) + (i % tiles_per_seq) * tm
    out = h_ref[...] + route[:, 2:3] * ybuf[slot, 0] + route[:, 3:4] * ybuf[slot, 1]
    o_ref[...] = jnp.where(rows >= PAD, out, 0.0)


def _combine(pos, y_rows, h, route, lp):
    t, d = h.shape
    tm = BLOCK
    return pl.pallas_call(
        functools.partial(_combine_kernel, tiles_per_seq=lp // tm),
        out_shape=jax.ShapeDtypeStruct((t, d), F32),
        grid_spec=pltpu.PrefetchScalarGridSpec(
            num_scalar_prefetch=1,
            grid=(t // tm,),
            in_specs=[pl.BlockSpec(memory_space=pl.ANY),
                      pl.BlockSpec((tm, d), lambda i, pos: (i, 0)),
                      pl.BlockSpec((tm, ROUTE_COLS), lambda i, pos: (i, 0))],
            out_specs=pl.BlockSpec((tm, d), lambda i, pos: (i, 0)),
            scratch_shapes=[pltpu.VMEM((2, 2, tm, d), F32), pltpu.SemaphoreType.DMA((2,))]),
        compiler_params=_cparams(("arbitrary",)),
        name="moe_combine",
    )(pos, y_rows, h, route)


def _dispatch(route, n_real_tokens):
    t = route.shape[0]
    e_flat = route[:, :2].astype(jnp.int32).reshape(t * 2)
    n_assign = n_real_tokens * 2
    n_blocks = -(-(n_assign + N_EXPERTS * (MOE_BLOCK - 1)) // MOE_BLOCK)
    onehot = (e_flat[:, None] == jnp.arange(N_EXPERTS, dtype=jnp.int32)[None, :]).astype(jnp.int32)
    ranks = jnp.cumsum(onehot, axis=0) - onehot
    counts = jnp.sum(onehot, axis=0)
    padded = ((counts + MOE_BLOCK - 1) // MOE_BLOCK) * MOE_BLOCK
    pend = jnp.cumsum(padded)
    pstart = pend - padded
    valid = e_flat < N_EXPERTS
    dest = jnp.sum(onehot * (ranks + pstart[None, :]), axis=1)
    p_rows = n_blocks * MOE_BLOCK
    tok = jnp.arange(t * 2, dtype=jnp.int32) // 2
    row_tok = jnp.zeros((p_rows,), jnp.int32).at[jnp.where(valid, dest, p_rows)].set(tok, mode="drop")
    blk_start = jnp.arange(n_blocks, dtype=jnp.int32) * MOE_BLOCK
    blk_expert = jnp.minimum(jnp.sum(blk_start[:, None] >= pend[None, :], axis=1), N_EXPERTS - 1)
    blk_expert = blk_expert.astype(jnp.int32)
    nused = (pend[-1] // MOE_BLOCK).astype(jnp.int32)
    pos = jnp.where(valid, dest, 0).astype(jnp.int32)
    blk = jnp.arange(n_blocks, dtype=jnp.int32)
    change = (blk == 0) | (blk_expert != jnp.roll(blk_expert, 1))
    first = change.astype(jnp.int32)
    change_at = jnp.where(change & (blk < nused), blk, n_blocks)
    next_change = lax.cummin(change_at, reverse=True)
    next_change = jnp.concatenate([next_change[1:], jnp.full((1,), n_blocks, jnp.int32)])
    nxt = jnp.where(next_change < n_blocks,
                    blk_expert[jnp.minimum(next_change, n_blocks - 1)], -1).astype(jnp.int32)
    return (blk_expert, first, nxt, row_tok, nused.reshape(1)), pos


def _pad_rows(w, rows, offset):
    out = jnp.zeros((rows, w.shape[1]), w.dtype)
    return out.at[offset:offset + w.shape[0]].set(w)


def kernel(x, meta_tokens, norm_mix, w_in, b_qkv, sinks, rwkv_mu, rwkv_w0, rwkv_w_decay_up, rwkv_a0, rwkv_w_iclr_up, rwkv_w_gate_up, rwkv_k_k, rwkv_k_a, rwkv_r_k, rwkv_ln_w, rwkv_ln_b, rwkv_vres_down, rwkv_vres_mu, rwkv_v0, rwkv_vres_up, w_branch_att, w_branch_rwkv, w_out, norm_ffn, w_coarse, b_coarse, w_fine, b_fine, w_exp_gate, w_exp_up, w_exp_down, norm_final):
    batch, seq, d = x.shape
    depth = w_in.shape[0]
    lp = PAD + N_META + seq
    t = batch * lp
    assert d == D_MODEL and lp % (8 * 16) == 0

    meta = jnp.broadcast_to(meta_tokens.astype(x.dtype)[None], (batch, N_META, d))
    h = jnp.concatenate([jnp.zeros((batch, PAD, d), x.dtype), meta, x], axis=1).reshape(t, d)
    tables = _rope_tables(lp)
    tm_proj = lp // 2
    tm_norm = lp // 8
    o_gate = QKV_DIM + GATE_DIM
    v_first = None
    for l in range(depth):
        u = _rmsnorm(h, norm_mix[l], tm_norm, BF16)
        w_l = w_in[l]
        w_qkv = w_l[:, :QKV_DIM].astype(BF16)
        w_gate = w_l[:, QKV_DIM:o_gate].astype(BF16)
        w_rkv = w_l[:, o_gate:o_gate + RKV_COLS]
        w_lora = w_l[:, o_gate + RKV_COLS:]
        mu = rwkv_mu[l]
        lora_parts = [w_lora]
        mu_parts = [mu]
        if l > 0:
            lora_parts.append(rwkv_vres_down[l - 1])
            mu_parts.append(rwkv_vres_mu[l - 1])
        n_lora = sum(p.shape[1] for p in lora_parts)
        lora_parts.append(jnp.zeros((d, LORA_COLS - n_lora), F32))
        mu_parts.append(jnp.zeros((LORA_COLS - n_lora,), F32))
        w_rwkv = jnp.concatenate([w_rkv] + lora_parts, axis=1).astype(BF16)
        mu_full = jnp.concatenate(mu_parts).reshape(1, RWKV_COLS)

        zqkv = _proj(u, w_qkv, b_qkv[l], tm_proj, None, F32, "proj_qkv")
        gates = _proj(u, w_gate, jnp.zeros((GATE_DIM,), F32), tm_proj, "sigmoid", BF16, "proj_gate")
        z_r = _proj(u, w_rwkv, jnp.zeros((RWKV_COLS,), F32), tm_proj, None, F32, "proj_rwkv")

        o_att = _attention(zqkv, sinks[l], tables, batch, lp)

        vecs = jnp.stack([rwkv_w0[l], rwkv_a0[l], rwkv_k_k[l], rwkv_k_a[l], rwkv_r_k[l].reshape(-1),
                          rwkv_ln_w[l], rwkv_ln_b[l],
                          rwkv_v0[l - 1] if l > 0 else jnp.zeros((RWKV_DIM,), F32)])
        wd = _pad_rows(rwkv_w_decay_up[l], LORA_COLS, 0).astype(BF16)
        wa = _pad_rows(rwkv_w_iclr_up[l], LORA_COLS, LORA).astype(BF16)
        wg = _pad_rows(rwkv_w_gate_up[l], LORA_COLS, 2 * LORA).astype(BF16)
        if l == 0:
            o_rwkv, v_first = _rwkv(z_r, mu_full, vecs, wd, wa, wg, None, None, batch, lp)
        else:
            wv = _pad_rows(rwkv_vres_up[l - 1], LORA_COLS, 3 * LORA).astype(BF16)
            o_rwkv = _rwkv(z_r, mu_full, vecs, wd, wa, wg, wv, v_first, batch, lp)

        w_route = jnp.concatenate(
            [w_coarse[l], w_fine[l].reshape(d, N_EXPERTS),
             jnp.zeros((d, ROUTE_COLS - N_GROUPS - N_EXPERTS), F32)], axis=1)
        b_route = jnp.concatenate(
            [b_coarse[l], b_fine[l].reshape(N_EXPERTS),
             jnp.zeros((ROUTE_COLS - N_GROUPS - N_EXPERTS,), F32)]).reshape(1, ROUTE_COLS)
        wr_hi = w_route.astype(BF16)
        wr_lo = (w_route - wr_hi.astype(F32)).astype(BF16)
        h, xn, route = _merge(o_att, o_rwkv, gates, h, w_branch_att[l].astype(BF16),
                              w_branch_rwkv[l].astype(BF16), w_out[l].astype(BF16), norm_ffn[l],
                              wr_hi, wr_lo, b_route, lp)

        dispatch, pos = _dispatch(route, batch * (N_META + seq))
        y_rows = _moe(dispatch, xn, w_exp_gate, w_exp_up, w_exp_down, l)
        h = _combine(pos, y_rows, h, route, lp)

    out = _rmsnorm(h, norm_final, tm_norm, x.dtype)
    return out.reshape(batch, lp, d)[:, PAD + N_META:]
```

```python
import functools

import jax
import jax.numpy as jnp
from jax import lax
from jax.experimental import pallas as pl
from jax.experimental.pallas import tpu as pltpu

F32 = jnp.float32
BF16 = jnp.bfloat16

D_MODEL = 2048
N_META = 16
NORM_EPS = 1e-5
HEAD_DIM = 64
ATT_HEADS = 16
ATT_KV_HEADS = 4
ATT_GROUP = ATT_HEADS // ATT_KV_HEADS
Q_DIM = ATT_HEADS * HEAD_DIM
KV_DIM = ATT_KV_HEADS * HEAD_DIM
QKV_DIM = Q_DIM + 2 * KV_DIM
BLOCK = 128
ROPE_DIM = HEAD_DIM // 4
ROPE_HALF = ROPE_DIM // 2
ROPE_THETA = 500000.0
RWKV_DIM = 1024
RWKV_HEADS = 16
LORA = 64
VRES_LORA = 32
GN_EPS = 64e-5
GATE_DIM = 2 * D_MODEL
N_GROUPS = 8
EXPERTS_PER_GROUP = 8
N_EXPERTS = N_GROUPS * EXPERTS_PER_GROUP
D_EXPERT = D_MODEL // 4
MOE_BLOCK = 128

PAD = BLOCK - N_META
LANES = 128
CHUNK = 64
N_PAIRS = RWKV_HEADS // 2
RWKV_SEQS = 4
RKV_COLS = 3 * RWKV_DIM
LORA_COLS = 256
RWKV_COLS = RKV_COLS + LORA_COLS
ROUTE_COLS = 128
VMEM_LIMIT = 56 * 1024 * 1024


def _cparams(sem):
    return pltpu.CompilerParams(dimension_semantics=sem, vmem_limit_bytes=VMEM_LIMIT)


def _dot(a, b):
    return jnp.dot(a.astype(BF16), b.astype(BF16), preferred_element_type=F32)


def _dot_nt(a, b):
    return lax.dot_general(a.astype(BF16), b.astype(BF16), (((1,), (1,)), ((), ())),
                           preferred_element_type=F32)


def _dot_tn(a, b):
    return lax.dot_general(a.astype(BF16), b.astype(BF16), (((0,), (0,)), ((), ())),
                           preferred_element_type=F32)


def _sigmoid(x):
    return 1.0 / (1.0 + jnp.exp(-x))


def _rmsnorm_kernel(h_ref, g_ref, o_ref):
    x = h_ref[...]
    y = x * lax.rsqrt(jnp.mean(x * x, axis=-1, keepdims=True) + NORM_EPS)
    o_ref[...] = (y * g_ref[...]).astype(o_ref.dtype)


def _rmsnorm(h, g, tm, out_dtype):
    t, d = h.shape
    return pl.pallas_call(
        _rmsnorm_kernel,
        out_shape=jax.ShapeDtypeStruct((t, d), out_dtype),
        grid=(t // tm,),
        in_specs=[pl.BlockSpec((tm, d), lambda i: (i, 0)),
                  pl.BlockSpec((1, d), lambda i: (0, 0))],
        out_specs=pl.BlockSpec((tm, d), lambda i: (i, 0)),
        compiler_params=_cparams(("parallel",)),
        name="rmsnorm",
    )(h, g.reshape(1, d))


def _final_norm(h, g, batch, lp):
    d = h.shape[1]
    nbk = lp // BLOCK
    return pl.pallas_call(
        _rmsnorm_kernel,
        out_shape=jax.ShapeDtypeStruct((batch * (nbk - 1) * BLOCK, d), h.dtype),
        grid=(batch, nbk - 1),
        in_specs=[pl.BlockSpec((BLOCK, d), lambda b, n: (b * nbk + n + 1, 0)),
                  pl.BlockSpec((1, d), lambda b, n: (0, 0))],
        out_specs=pl.BlockSpec((BLOCK, d), lambda b, n: (b * (nbk - 1) + n, 0)),
        compiler_params=_cparams(("parallel", "parallel")),
        name="final_norm",
    )(h, g.reshape(1, d))


def _proj_kernel(u_ref, w_ref, b_ref, o_ref, *, act):
    z = jnp.dot(u_ref[...], w_ref[...], preferred_element_type=F32) + b_ref[...]
    if act == "sigmoid":
        z = _sigmoid(z)
    o_ref[...] = z.astype(o_ref.dtype)


def _proj(u, w, b, tm, act, out_dtype, name):
    t, d = u.shape
    n = w.shape[1]
    tn = 512 if n % 512 == 0 else 256
    return pl.pallas_call(
        functools.partial(_proj_kernel, act=act),
        out_shape=jax.ShapeDtypeStruct((t, n), out_dtype),
        grid=(t // tm, n // tn),
        in_specs=[pl.BlockSpec((tm, d), lambda i, j: (i, 0)),
                  pl.BlockSpec((d, tn), lambda i, j: (0, j)),
                  pl.BlockSpec((1, tn), lambda i, j: (0, j))],
        out_specs=pl.BlockSpec((tm, tn), lambda i, j: (i, j)),
        compiler_params=_cparams(("parallel", "arbitrary")),
        name=name,
    )(u, w, b.reshape(1, n))


def _rope(x, tab_ref):
    cos, sin_lo, sin_hi = tab_ref[0], tab_ref[1], tab_ref[2]
    outs = []
    for c in range(x.shape[1] // LANES):
        xc = x[:, c * LANES:(c + 1) * LANES]
        up = pltpu.roll(xc, LANES - ROPE_HALF, axis=1)
        dn = pltpu.roll(xc, ROPE_HALF, axis=1)
        outs.append(xc * cos + up * sin_lo + dn * sin_hi)
    return jnp.concatenate(outs, axis=1)


def _attn_kernel(sink_ref, q_ref, kc_ref, kp_ref, km_ref, vc_ref, vp_ref, vm_ref,
                 tc_ref, tp_ref, tm_ref, o_ref):
    n = pl.program_id(1)
    q = _rope(q_ref[...], tc_ref) * (HEAD_DIM ** -0.5)
    keys = jnp.concatenate([_rope(kp_ref[...], tp_ref), _rope(kc_ref[...], tc_ref),
                            _rope(km_ref[...], tm_ref)], axis=0).astype(BF16)
    vals = jnp.concatenate([vp_ref[...], vc_ref[...], vm_ref[...]], axis=0).astype(BF16)
    q = q.astype(BF16)

    rows = ATT_GROUP * BLOCK
    r = lax.broadcasted_iota(jnp.int32, (rows, 3 * BLOCK), 0) & (BLOCK - 1)
    c = lax.broadcasted_iota(jnp.int32, (rows, 3 * BLOCK), 1)
    band = (c > r) & (c <= r + BLOCK) & (c + BLOCK * n >= 2 * BLOCK)
    m = c - 2 * BLOCK
    meta = (m >= PAD) & (m <= r + n * BLOCK)
    ok = ((c < 2 * BLOCK) & band) | ((c >= 2 * BLOCK) & meta)
    g_of_row = lax.broadcasted_iota(jnp.int32, (rows, 1), 0) >> (BLOCK.bit_length() - 1)

    outs = []
    for h in range(ATT_KV_HEADS):
        qg = jnp.concatenate(
            [q[:, (h * ATT_GROUP + g) * HEAD_DIM:(h * ATT_GROUP + g + 1) * HEAD_DIM]
             for g in range(ATT_GROUP)], axis=0)
        kh = keys[:, h * HEAD_DIM:(h + 1) * HEAD_DIM]
        vh = vals[:, h * HEAD_DIM:(h + 1) * HEAD_DIM]
        s = lax.dot_general(qg, kh, (((1,), (1,)), ((), ())), preferred_element_type=F32)
        s = jnp.where(ok, s, -1e30)
        sink = jnp.zeros((rows, 1), F32)
        for g in range(ATT_GROUP):
            sink = jnp.where(g_of_row == g, sink_ref[h * ATT_GROUP + g], sink)
        mx = jnp.maximum(jnp.max(s, axis=-1, keepdims=True), sink)
        e = jnp.exp(s - mx)
        den = jnp.sum(e, axis=-1, keepdims=True) + jnp.exp(sink - mx)
        p = (e / den).astype(BF16)
        og = jnp.dot(p, vh, preferred_element_type=F32)
        outs.extend(og[g * BLOCK:(g + 1) * BLOCK] for g in range(ATT_GROUP))
    o_ref[...] = jnp.concatenate(outs, axis=1).astype(o_ref.dtype)


def _rope_tables(lp):
    pos = (jnp.arange(lp) - PAD).astype(F32)
    inv = jnp.power(jnp.float32(ROPE_THETA), -jnp.arange(ROPE_HALF, dtype=F32) / ROPE_HALF)
    ang = pos[:, None] * inv[None, :]
    cos, sin = jnp.cos(ang), jnp.sin(ang)
    zeros = jnp.zeros((lp, HEAD_DIM - ROPE_DIM), F32)
    zh = jnp.zeros((lp, ROPE_HALF), F32)
    c_tab = jnp.concatenate([cos, cos, jnp.ones_like(zeros)], axis=1)
    lo_tab = jnp.concatenate([-sin, zh, zeros], axis=1)
    hi_tab = jnp.concatenate([zh, sin, zeros], axis=1)
    tab = jnp.stack([c_tab, lo_tab, hi_tab])
    return jnp.tile(tab, (1, 1, LANES // HEAD_DIM))


def _attention(zqkv, sinks, tables, batch, lp):
    nb = lp // BLOCK
    qcols = Q_DIM // KV_DIM
    cur = lambda b, n: b * nb + n
    prev = lambda b, n: b * nb + jnp.maximum(n - 1, 0)
    first = lambda b, n: b * nb
    kv_spec = lambda rowf, col: pl.BlockSpec((BLOCK, KV_DIM), lambda b, n: (rowf(b, n), col))
    tab_spec = lambda f: pl.BlockSpec((3, BLOCK, LANES), lambda b, n: (0, f(b, n), 0))
    return pl.pallas_call(
        _attn_kernel,
        out_shape=jax.ShapeDtypeStruct((batch * lp, Q_DIM), BF16),
        grid=(batch, nb),
        in_specs=[pl.BlockSpec(memory_space=pltpu.SMEM),
                  pl.BlockSpec((BLOCK, Q_DIM), lambda b, n: (cur(b, n), 0)),
                  kv_spec(cur, qcols), kv_spec(prev, qcols), kv_spec(first, qcols),
                  kv_spec(cur, qcols + 1), kv_spec(prev, qcols + 1), kv_spec(first, qcols + 1),
                  tab_spec(lambda b, n: n), tab_spec(lambda b, n: jnp.maximum(n - 1, 0)),
                  tab_spec(lambda b, n: 0)],
        out_specs=pl.BlockSpec((BLOCK, Q_DIM), lambda b, n: (cur(b, n), 0)),
        compiler_params=_cparams(("parallel", "arbitrary")),
        name="swa_attention",
    )(sinks, zqkv, zqkv, zqkv, zqkv, zqkv, zqkv, zqkv, tables, tables, tables)


def _block_diag(p):
    lane = lax.broadcasted_iota(jnp.int32, p.shape, 1)
    return jnp.concatenate([jnp.where(lane < HEAD_DIM, p, 0.0),
                            jnp.where(lane >= HEAD_DIM, p, 0.0)], axis=0)


def _softplus(x):
    return jnp.maximum(x, 0.0) + jnp.log(1.0 + jnp.exp(-jnp.abs(x)))


def _rwkv_kernel(*refs, has_vres):
    if has_vres:
        (z_ref, zprev_ref, mu_ref, vec_ref, wd_ref, wa_ref, wg_ref, wv_ref, vfirst_ref,
         o_ref, h_scr) = refs
    else:
        (z_ref, zprev_ref, mu_ref, vec_ref, wd_ref, wa_ref, wg_ref,
         o_ref, vfirst_out_ref, h_scr) = refs
    c = pl.program_id(1)
    nb = z_ref.shape[0]
    rows = nb * CHUNK
    seqs = range(nb)

    @pl.when(c == 0)
    def _():
        h_scr[...] = jnp.zeros_like(h_scr)

    z = z_ref[...].reshape(rows, RWKV_COLS)
    row = lax.broadcasted_iota(jnp.int32, (rows, 1), 0)
    trow = row & (CHUNK - 1)
    by_seq = lambda vals: functools.reduce(
        lambda acc, b: jnp.where(row >= b * CHUNK, vals[b], acc), seqs[1:], vals[0])
    not_first = jnp.where(c > 0, 1.0, 0.0)
    prev_rows = by_seq([zprev_ref[b, 7:8, :] * not_first for b in seqs])
    z_shift = jnp.where(trow == 0, prev_rows, pltpu.roll(z, 1, axis=0))
    zs = z + (z_shift - z) * mu_ref[...]

    w0, a0, k_k, k_a, r_k, ln_w, ln_b, v0 = (vec_ref[i:i + 1, :] for i in range(8))
    r = zs[:, :RWKV_DIM]
    k = zs[:, RWKV_DIM:2 * RWKV_DIM]
    v = zs[:, 2 * RWKV_DIM:RKV_COLS]
    lora = zs[:, RKV_COLS:]
    w_log = -_softplus(-(w0 + _dot(jnp.tanh(lora), wd_ref[...]))) - 0.5
    lw = -jnp.exp(w_log)
    a_sig = _sigmoid(a0 + _dot(lora, wa_ref[...]))
    gate = _dot(_sigmoid(lora), wg_ref[...])
    if has_vres:
        v_first = vfirst_ref[...].reshape(rows, RWKV_DIM)
        v = v + (v_first - v) * _sigmoid(v0 + _dot(lora, wv_ref[...]))
    else:
        vfirst_out_ref[...] = v.reshape(nb, CHUNK, RWKV_DIM)

    same_head = ((lax.broadcasted_iota(jnp.int32, (LANES, LANES), 0) >= HEAD_DIM)
                 == (lax.broadcasted_iota(jnp.int32, (LANES, LANES), 1) >= HEAD_DIM))
    ones_wide = ((lax.broadcasted_iota(jnp.int32, (2 * LANES, 2 * LANES), 0) >> 6)
                 == (lax.broadcasted_iota(jnp.int32, (2 * LANES, 2 * LANES), 1) >> 6)).astype(BF16)

    def head_sums(xs):
        wide = 2 * LANES
        n_col = RWKV_DIM // wide
        stacked = jnp.concatenate([x[:, q * wide:(q + 1) * wide] for x in xs for q in range(n_col)],
                                  axis=0)
        sums = _dot(stacked, ones_wide)
        return [jnp.concatenate([sums[(i * n_col + q) * rows:(i * n_col + q + 1) * rows]
                                 for q in range(n_col)], axis=1) for i in range(len(xs))]

    kk = k * k_k
    k = k * (1.0 + (a_sig - 1.0) * k_a)
    ssq, bonus = head_sums([kk * kk, r * k * r_k])
    kk = kk / jnp.maximum(jnp.sqrt(ssq), 1e-12)
    a_vec = -kk
    b_vec = kk * a_sig

    cum = lw
    sh = 1
    while sh < CHUNK:
        cum = cum + jnp.where(trow >= sh, pltpu.roll(cum, sh, axis=0), 0.0)
        sh *= 2
    e_tot = [jnp.exp(cum[(b + 1) * CHUNK - 1:(b + 1) * CHUNK, :]) for b in seqs]
    e_tot_rows = by_seq(e_tot)
    e_pos = jnp.exp(cum)
    e_neg = jnp.exp(-cum)
    a_t = a_vec * jnp.exp(cum - lw)
    r_t = r * e_pos
    b_t = b_vec * e_neg
    k_t = k * e_neg
    b_h = b_t * e_tot_rows
    k_h = k_t * e_tot_rows

    t_idx = lax.broadcasted_iota(jnp.int32, (CHUNK, LANES), 0)
    s_idx = lax.broadcasted_iota(jnp.int32, (CHUNK, LANES), 1) & (HEAD_DIM - 1)
    strict = t_idx > s_idx
    incl = t_idx >= s_idx
    eye = (t_idx == s_idx).astype(F32)
    diag_mask = (lax.broadcasted_iota(jnp.int32, (LANES, LANES), 0)
                 == lax.broadcasted_iota(jnp.int32, (LANES, LANES), 1))

    units = [(b, p) for b in seqs for p in range(N_PAIRS)]
    sl = lambda x, u: x[u[0] * CHUNK:(u[0] + 1) * CHUNK, u[1] * LANES:(u[1] + 1) * LANES]
    at = [sl(a_t, u) for u in units]
    rt = [sl(r_t, u) for u in units]
    vp = [sl(v, u) for u in units]
    idx = range(len(units))

    a_ab, a_ak, m_rb, m_rk = [], [], [], []
    for i, u in enumerate(units):
        lhs = jnp.concatenate([at[i], rt[i]], axis=0)
        sc_b = _dot_nt(lhs, _block_diag(sl(b_t, u)))
        sc_k = _dot_nt(lhs, _block_diag(sl(k_t, u)))
        a_ab.append(jnp.where(strict, sc_b[:CHUNK], 0.0))
        a_ak.append(jnp.where(strict, sc_k[:CHUNK], 0.0))
        m_rb.append(jnp.where(incl, sc_b[CHUNK:], 0.0))
        m_rk.append(jnp.where(incl, sc_k[CHUNK:], 0.0))

    pw = [_dot(a_ab[i], _block_diag(a_ab[i])) for i in idx]
    tm = [eye + a_ab[i] for i in idx]
    n_stage = CHUNK.bit_length() - 2
    for stage in range(n_stage):
        last = stage == n_stage - 1
        for i in idx:
            if last:
                tm[i] = tm[i] + _dot(pw[i], _block_diag(tm[i]))
            else:
                res = _dot(pw[i], jnp.concatenate([_block_diag(pw[i]), _block_diag(tm[i])], axis=1))
                pw[i] = res[:, :LANES]
                tm[i] = tm[i] + res[:, LANES:]

    h0 = [h_scr[i] for i in idx]
    zeros = jnp.zeros((CHUNK, LANES), F32)
    xv = [_dot(jnp.concatenate([a_ak[i], m_rk[i]], axis=0), _block_diag(vp[i])) for i in idx]
    wu = [_dot(tm[i], jnp.concatenate([_block_diag(at[i]), _block_diag(xv[i][:CHUNK])], axis=1))
          for i in idx]
    qo = [_dot(m_rb[i], jnp.concatenate([_block_diag(wu[i][:, :LANES]),
                                         _block_diag(wu[i][:, LANES:])], axis=1)) for i in idx]
    gh = [_dot_tn(jnp.concatenate([sl(b_h, u), sl(k_h, u)], axis=0),
                  jnp.concatenate([wu[i], jnp.concatenate([zeros, vp[i]], axis=1)], axis=0))
          for i, u in enumerate(units)]
    outs = []
    for i, (b, p) in enumerate(units):
        q_hat = rt[i] + qo[i][:, :LANES]
        g_m = (jnp.where(same_head, gh[i][:, :LANES], 0.0)
               + jnp.where(diag_mask, e_tot[b][:, p * LANES:(p + 1) * LANES], 0.0))
        ser = _dot(jnp.concatenate([q_hat, g_m], axis=0), h0[i])
        outs.append(ser[:CHUNK] + qo[i][:, LANES:] + xv[i][CHUNK:])
        h_scr[i] = ser[CHUNK:] + jnp.where(same_head, gh[i][:, LANES:], 0.0)
    y = jnp.concatenate([jnp.concatenate(outs[b * N_PAIRS:(b + 1) * N_PAIRS], axis=1)
                         for b in seqs], axis=0)

    inv_n = 1.0 / HEAD_DIM
    yc = y - head_sums([y])[0] * inv_n
    var = head_sums([yc * yc])[0] * inv_n
    yn = yc * lax.rsqrt(var + GN_EPS) * ln_w + ln_b
    o_ref[...] = ((yn + bonus * v) * gate).astype(o_ref.dtype).reshape(nb, CHUNK, RWKV_DIM)


def _rwkv(z_r, mu, vecs, wd, wa, wg, wv, v_first, batch, lp):
    nc = lp // CHUNK
    nb = RWKV_SEQS if batch % RWKV_SEQS == 0 else 1
    has_vres = v_first is not None
    const = lambda shape: pl.BlockSpec(shape, lambda b, c: (0,) * len(shape))
    chunk_spec = lambda w: pl.BlockSpec((nb, CHUNK, w), lambda b, c: (b, c, 0))
    in_specs = [chunk_spec(RWKV_COLS),
                pl.BlockSpec((nb, 8, RWKV_COLS),
                             lambda b, c: (b, jnp.maximum(c * (CHUNK // 8) - 1, 0), 0)),
                const((1, RWKV_COLS)), const((8, RWKV_DIM)),
                const((LORA_COLS, RWKV_DIM)), const((LORA_COLS, RWKV_DIM)),
                const((LORA_COLS, RWKV_DIM))]
    z3 = z_r.reshape(batch, lp, RWKV_COLS)
    args = [z3, z3, mu, vecs, wd, wa, wg]
    o_shape = jax.ShapeDtypeStruct((batch, lp, RWKV_DIM), BF16)
    if has_vres:
        in_specs += [const((LORA_COLS, RWKV_DIM)), chunk_spec(RWKV_DIM)]
        args += [wv, v_first]
        out_shape, out_specs = o_shape, chunk_spec(RWKV_DIM)
    else:
        out_shape = (o_shape, jax.ShapeDtypeStruct((batch, lp, RWKV_DIM), F32))
        out_specs = (chunk_spec(RWKV_DIM), chunk_spec(RWKV_DIM))
    out = pl.pallas_call(
        functools.partial(_rwkv_kernel, has_vres=has_vres),
        out_shape=out_shape,
        grid=(batch // nb, nc),
        in_specs=in_specs,
        out_specs=out_specs,
        scratch_shapes=[pltpu.VMEM((nb * N_PAIRS, LANES, LANES), F32)],
        compiler_params=_cparams(("parallel", "arbitrary")),
        name="rwkv7_mix",
    )(*args)
    if has_vres:
        return out.reshape(batch * lp, RWKV_DIM)
    return out[0].reshape(batch * lp, RWKV_DIM), out[1]


def _merge_kernel(oa_ref, orw_ref, ga_ref, gr_ref, h_ref, pa_ref, pb_ref, wo_ref, nf_ref,
                  wrh_ref, wrl_ref, br_ref, hn_ref, xn_ref, route_ref, *, tiles_per_seq):
    i = pl.program_id(0)
    tm = h_ref.shape[0]
    ya = jnp.dot(oa_ref[...], pa_ref[...], preferred_element_type=F32)
    yb = jnp.dot(orw_ref[...], pb_ref[...], preferred_element_type=F32)
    mixed = ga_ref[...].astype(F32) * ya + gr_ref[...].astype(F32) * yb
    upd = jnp.dot(mixed.astype(BF16), wo_ref[...], preferred_element_type=F32)
    rows = lax.broadcasted_iota(jnp.int32, (tm, 1), 0) + (i % tiles_per_seq) * tm
    real = rows >= PAD
    h = jnp.where(real, h_ref[...] + upd, 0.0)
    hn_ref[...] = h
    xn = h * lax.rsqrt(jnp.mean(h * h, axis=-1, keepdims=True) + NORM_EPS) * nf_ref[...]
    xn_ref[...] = xn

    x_hi = xn.astype(BF16)
    x_lo = (xn - x_hi.astype(F32)).astype(BF16)
    logits = (jnp.dot(x_hi, wrh_ref[...], preferred_element_type=F32)
              + jnp.dot(x_hi, wrl_ref[...], preferred_element_type=F32)
              + jnp.dot(x_lo, wrh_ref[...], preferred_element_type=F32)) + br_ref[...]
    col = lax.broadcasted_iota(jnp.int32, logits.shape, 1).astype(F32)
    neg = -jnp.inf
    big = float(ROUTE_COLS)
    coarse = jnp.where(col < N_GROUPS, logits, neg)
    c_max = jnp.max(coarse, axis=-1, keepdims=True)
    grp = jnp.min(jnp.where(coarse == c_max, col, big), axis=-1, keepdims=True)
    p_grp = 1.0 / jnp.sum(jnp.exp(coarse - c_max), axis=-1, keepdims=True)
    lo_col = N_GROUPS + grp * EXPERTS_PER_GROUP
    fine = jnp.where((col >= lo_col) & (col < lo_col + EXPERTS_PER_GROUP), logits, neg)
    f1 = jnp.max(fine, axis=-1, keepdims=True)
    i1 = jnp.min(jnp.where(fine == f1, col, big), axis=-1, keepdims=True)
    fine2 = jnp.where(col == i1, neg, fine)
    f2 = jnp.max(fine2, axis=-1, keepdims=True)
    i2 = jnp.min(jnp.where(fine2 == f2, col, big), axis=-1, keepdims=True)
    e21 = jnp.exp(f2 - f1)
    w1 = p_grp / (1.0 + e21)
    w2 = p_grp * e21 / (1.0 + e21)
    e1 = jnp.where(real, i1 - N_GROUPS, N_EXPERTS).astype(F32)
    e2 = jnp.where(real, i2 - N_GROUPS, N_EXPERTS).astype(F32)
    w1 = jnp.where(real, w1, 0.0)
    w2 = jnp.where(real, w2, 0.0)
    route_ref[...] = jnp.where(col == 0, e1, jnp.where(col == 1, e2,
                               jnp.where(col == 2, w1, jnp.where(col == 3, w2, 0.0))))


def _merge(o_att, o_rwkv, gates, h, pa, pb, wo, nf, wr_hi, wr_lo, br, lp):
    t, d = h.shape
    tiles_per_seq = 8
    tm = lp // tiles_per_seq
    row = lambda w: pl.BlockSpec((tm, w), lambda i: (i, 0))
    const = lambda shape: pl.BlockSpec(shape, lambda i: (0,) * len(shape))
    return pl.pallas_call(
        functools.partial(_merge_kernel, tiles_per_seq=tiles_per_seq),
        out_shape=(jax.ShapeDtypeStruct((t, d), F32), jax.ShapeDtypeStruct((t, d), F32),
                   jax.ShapeDtypeStruct((t, ROUTE_COLS), F32)),
        grid=(t // tm,),
        in_specs=[row(Q_DIM), row(RWKV_DIM),
                  pl.BlockSpec((tm, d), lambda i: (i, 0)), pl.BlockSpec((tm, d), lambda i: (i, 1)),
                  row(d), const((Q_DIM, d)), const((RWKV_DIM, d)), const((d, d)), const((1, d)),
                  const((d, ROUTE_COLS)), const((d, ROUTE_COLS)), const((1, ROUTE_COLS))],
        out_specs=(row(d), row(d), row(ROUTE_COLS)),
        compiler_params=_cparams(("parallel",)),
        name="merge_route",
    )(o_att, o_rwkv, gates, gates, h, pa, pb, wo, nf.reshape(1, d), wr_hi, wr_lo, br)


def _moe_kernel(blk_expert_ref, first_ref, next_ref, row_tok_ref, nused_ref, x_hbm,
                wg_hbm, wu_hbm, wd_hbm, y_ref, xbuf, wg_f, wu_f, wd_f, wg_s, wu_s, wd_s,
                xsem, wsem, *, layer):
    i = pl.program_id(0)
    n_blocks = pl.num_programs(0)
    nused = nused_ref[0]
    slot = i % 2

    def weight_copies(e):
        return (pltpu.make_async_copy(wg_hbm.at[layer, e], wg_f, wsem.at[0]),
                pltpu.make_async_copy(wu_hbm.at[layer, e], wu_f, wsem.at[1]),
                pltpu.make_async_copy(wd_hbm.at[layer, e], wd_f, wsem.at[2]))

    def gather(block, slot_):
        for r in range(MOE_BLOCK):
            tok = row_tok_ref[block * MOE_BLOCK + r]
            pltpu.make_async_copy(x_hbm.at[pl.ds(tok, 1)], xbuf.at[slot_, pl.ds(r, 1)],
                                  xsem.at[slot_]).start()

    def wait_rows(slot_):
        pltpu.make_async_copy(x_hbm.at[pl.ds(0, MOE_BLOCK)], xbuf.at[slot_], xsem.at[slot_]).wait()

    @pl.when((i == 0) & (nused > 0))
    def _():
        for cp in weight_copies(blk_expert_ref[0]):
            cp.start(priority=1)
        gather(0, 0)

    @pl.when((i < nused) & (first_ref[i] == 1))
    def _():
        for cp in weight_copies(blk_expert_ref[i]):
            cp.wait()
        wg_s[...] = wg_f[...].astype(BF16)
        wu_s[...] = wu_f[...].astype(BF16)
        wd_s[...] = wd_f[...].astype(BF16)

        @pl.when(next_ref[i] >= 0)
        def _():
            for cp in weight_copies(next_ref[i]):
                cp.start(priority=1)

    @pl.when(i < nused)
    def _():
        wait_rows(slot)
        gather(jnp.minimum(i + 1, n_blocks - 1), 1 - slot)
        xb = xbuf[slot].astype(BF16)
        g = jnp.dot(xb, wg_s[...], preferred_element_type=F32)
        u = jnp.dot(xb, wu_s[...], preferred_element_type=F32)
        hid = (g * _sigmoid(g) * u).astype(BF16)
        y_ref[...] = jnp.dot(hid, wd_s[...], preferred_element_type=F32)

        @pl.when(i == n_blocks - 1)
        def _():
            wait_rows(1 - slot)

    @pl.when(i >= nused)
    def _():
        y_ref[...] = jnp.zeros_like(y_ref)

        @pl.when((i == nused) & (nused > 0))
        def _():
            wait_rows(slot)


def _moe(dispatch, xn, w_gate, w_up, w_down, layer):
    blk_expert, first, nxt, row_tok, nused = dispatch
    n_blocks = blk_expert.shape[0]
    d = xn.shape[1]
    any_spec = pl.BlockSpec(memory_space=pl.ANY)
    return pl.pallas_call(
        functools.partial(_moe_kernel, layer=layer),
        out_shape=jax.ShapeDtypeStruct((n_blocks * MOE_BLOCK, d), F32),
        grid_spec=pltpu.PrefetchScalarGridSpec(
            num_scalar_prefetch=5,
            grid=(n_blocks,),
            in_specs=[any_spec, any_spec, any_spec, any_spec],
            out_specs=pl.BlockSpec((MOE_BLOCK, d), lambda i, *_: (i, 0)),
            scratch_shapes=[pltpu.VMEM((2, MOE_BLOCK, d), F32),
                            pltpu.VMEM((d, D_EXPERT), F32), pltpu.VMEM((d, D_EXPERT), F32),
                            pltpu.VMEM((D_EXPERT, d), F32),
                            pltpu.VMEM((d, D_EXPERT), BF16), pltpu.VMEM((d, D_EXPERT), BF16),
                            pltpu.VMEM((D_EXPERT, d), BF16),
                            pltpu.SemaphoreType.DMA((2,)), pltpu.SemaphoreType.DMA((3,))]),
        compiler_params=_cparams(("arbitrary",)),
        name="moe_experts",
    )(blk_expert, first, nxt, row_tok, nused, xn, w_gate, w_up, w_down)


def _combine_kernel(pos_ref, y_hbm, h_ref, route_ref, o_ref, ybuf, sem, *, tiles_per_seq):
    i = pl.program_id(0)
    n = pl.num_programs(0)
    tm = h_ref.shape[0]
    slot = i % 2

    def gather(tile, slot_):
        for r in range(tm):
            for j in range(2):
                p = pos_ref[(tile * tm + r) * 2 + j]
                pltpu.make_async_copy(y_hbm.at[pl.ds(p, 1)], ybuf.at[slot_, j, pl.ds(r, 1)],
                                      sem.at[slot_]).start(priority=j)

    @pl.when(i == 0)
    def _():
        gather(0, 0)

    @pl.when(i + 1 < n)
    def _():
        gather(i + 1, 1 - slot)

    for j in range(2):
        pltpu.make_async_copy(y_hbm.at[pl.ds(0, tm)], ybuf.at[slot, j], sem.at[slot]).wait()
    route = route_ref[...]
    rows = lax.broadcasted_iota(jnp.int32, (tm, 1), 0) + (i % tiles_per_seq) * tm
    out = h_ref[...] + route[:, 2:3] * ybuf[slot, 0] + route[:, 3:4] * ybuf[slot, 1]
    o_ref[...] = jnp.where(rows >= PAD, out, 0.0)


def _combine(pos, y_rows, h, route, lp):
    t, d = h.shape
    tm = BLOCK
    return pl.pallas_call(
        functools.partial(_combine_kernel, tiles_per_seq=lp // tm),
        out_shape=jax.ShapeDtypeStruct((t, d), F32),
        grid_spec=pltpu.PrefetchScalarGridSpec(
            num_scalar_prefetch=1,
            grid=(t // tm,),
            in_specs=[pl.BlockSpec(memory_space=pl.ANY),
                      pl.BlockSpec((tm, d), lambda i, pos: (i, 0)),
                      pl.BlockSpec((tm, ROUTE_COLS), lambda i, pos: (i, 0))],
            out_specs=pl.BlockSpec((tm, d), lambda i, pos: (i, 0)),
            scratch_shapes=[pltpu.VMEM((2, 2, tm, d), F32), pltpu.SemaphoreType.DMA((2,))]),
        compiler_params=_cparams(("arbitrary",)),
        name="moe_combine",
    )(pos, y_rows, h, route)


def _dispatch(route, n_real_tokens):
    t = route.shape[0]
    e_flat = route[:, :2].astype(jnp.int32).reshape(t * 2)
    n_assign = n_real_tokens * 2
    n_blocks = -(-(n_assign + N_EXPERTS * (MOE_BLOCK - 1)) // MOE_BLOCK)
    onehot = (e_flat[:, None] == jnp.arange(N_EXPERTS, dtype=jnp.int32)[None, :]).astype(jnp.int32)
    ranks = jnp.cumsum(onehot, axis=0) - onehot
    counts = jnp.sum(onehot, axis=0)
    padded = ((counts + MOE_BLOCK - 1) // MOE_BLOCK) * MOE_BLOCK
    pend = jnp.cumsum(padded)
    pstart = pend - padded
    valid = e_flat < N_EXPERTS
    dest = jnp.sum(onehot * (ranks + pstart[None, :]), axis=1)
    p_rows = n_blocks * MOE_BLOCK
    tok = jnp.arange(t * 2, dtype=jnp.int32) // 2
    row_tok = jnp.zeros((p_rows,), jnp.int32).at[jnp.where(valid, dest, p_rows)].set(tok, mode="drop")
    blk_start = jnp.arange(n_blocks, dtype=jnp.int32) * MOE_BLOCK
    blk_expert = jnp.minimum(jnp.sum(blk_start[:, None] >= pend[None, :], axis=1), N_EXPERTS - 1)
    blk_expert = blk_expert.astype(jnp.int32)
    nused = (pend[-1] // MOE_BLOCK).astype(jnp.int32)
    pos = jnp.where(valid, dest, 0).astype(jnp.int32)
    blk = jnp.arange(n_blocks, dtype=jnp.int32)
    change = (blk == 0) | (blk_expert != jnp.roll(blk_expert, 1))
    first = change.astype(jnp.int32)
    change_at = jnp.where(change & (blk < nused), blk, n_blocks)
    next_change = lax.cummin(change_at, reverse=True)
    next_change = jnp.concatenate([next_change[1:], jnp.full((1,), n_blocks, jnp.int32)])
    nxt = jnp.where(next_change < n_blocks,
                    blk_expert[jnp.minimum(next_change, n_blocks - 1)], -1).astype(jnp.int32)
    return (blk_expert, first, nxt, row_tok, nused.reshape(1)), pos


def _pad_rows(w, rows, offset):
    out = jnp.zeros((rows, w.shape[1]), w.dtype)
    return out.at[offset:offset + w.shape[0]].set(w)


def kernel(x, meta_tokens, norm_mix, w_in, b_qkv, sinks, rwkv_mu, rwkv_w0, rwkv_w_decay_up, rwkv_a0, rwkv_w_iclr_up, rwkv_w_gate_up, rwkv_k_k, rwkv_k_a, rwkv_r_k, rwkv_ln_w, rwkv_ln_b, rwkv_vres_down, rwkv_vres_mu, rwkv_v0, rwkv_vres_up, w_branch_att, w_branch_rwkv, w_out, norm_ffn, w_coarse, b_coarse, w_fine, b_fine, w_exp_gate, w_exp_up, w_exp_down, norm_final):
    batch, seq, d = x.shape
    depth = w_in.shape[0]
    lp = PAD + N_META + seq
    t = batch * lp
    assert d == D_MODEL and lp % (8 * 16) == 0

    meta = jnp.broadcast_to(meta_tokens.astype(x.dtype)[None], (batch, N_META, d))
    h = jnp.concatenate([jnp.zeros((batch, PAD, d), x.dtype), meta, x], axis=1).reshape(t, d)
    tables = _rope_tables(lp)
    tm_proj = lp // 2
    tm_norm = lp // 8
    o_gate = QKV_DIM + GATE_DIM
    v_first = None
    for l in range(depth):
        u = _rmsnorm(h, norm_mix[l], tm_norm, BF16)
        w_l = w_in[l]
        w_qkv = w_l[:, :QKV_DIM].astype(BF16)
        w_gate = w_l[:, QKV_DIM:o_gate].astype(BF16)
        w_rkv = w_l[:, o_gate:o_gate + RKV_COLS]
        w_lora = w_l[:, o_gate + RKV_COLS:]
        mu = rwkv_mu[l]
        lora_parts = [w_lora]
        mu_parts = [mu]
        if l > 0:
            lora_parts.append(rwkv_vres_down[l - 1])
            mu_parts.append(rwkv_vres_mu[l - 1])
        n_lora = sum(p.shape[1] for p in lora_parts)
        lora_parts.append(jnp.zeros((d, LORA_COLS - n_lora), F32))
        mu_parts.append(jnp.zeros((LORA_COLS - n_lora,), F32))
        w_rwkv = jnp.concatenate([w_rkv] + lora_parts, axis=1).astype(BF16)
        mu_full = jnp.concatenate(mu_parts).reshape(1, RWKV_COLS)

        zqkv = _proj(u, w_qkv, b_qkv[l], tm_proj, None, F32, "proj_qkv")
        gates = _proj(u, w_gate, jnp.zeros((GATE_DIM,), F32), tm_proj, "sigmoid", BF16, "proj_gate")
        z_r = _proj(u, w_rwkv, jnp.zeros((RWKV_COLS,), F32), tm_proj, None, F32, "proj_rwkv")

        o_att = _attention(zqkv, sinks[l], tables, batch, lp)

        vecs = jnp.stack([rwkv_w0[l], rwkv_a0[l], rwkv_k_k[l], rwkv_k_a[l], rwkv_r_k[l].reshape(-1),
                          rwkv_ln_w[l], rwkv_ln_b[l],
                          rwkv_v0[l - 1] if l > 0 else jnp.zeros((RWKV_DIM,), F32)])
        wd = _pad_rows(rwkv_w_decay_up[l], LORA_COLS, 0).astype(BF16)
        wa = _pad_rows(rwkv_w_iclr_up[l], LORA_COLS, LORA).astype(BF16)
        wg = _pad_rows(rwkv_w_gate_up[l], LORA_COLS, 2 * LORA).astype(BF16)
        if l == 0:
            o_rwkv, v_first = _rwkv(z_r, mu_full, vecs, wd, wa, wg, None, None, batch, lp)
        else:
            wv = _pad_rows(rwkv_vres_up[l - 1], LORA_COLS, 3 * LORA).astype(BF16)
            o_rwkv = _rwkv(z_r, mu_full, vecs, wd, wa, wg, wv, v_first, batch, lp)

        w_route = jnp.concatenate(
            [w_coarse[l], w_fine[l].reshape(d, N_EXPERTS),
             jnp.zeros((d, ROUTE_COLS - N_GROUPS - N_EXPERTS), F32)], axis=1)
        b_route = jnp.concatenate(
            [b_coarse[l], b_fine[l].reshape(N_EXPERTS),
             jnp.zeros((ROUTE_COLS - N_GROUPS - N_EXPERTS,), F32)]).reshape(1, ROUTE_COLS)
        wr_hi = w_route.astype(BF16)
        wr_lo = (w_route - wr_hi.astype(F32)).astype(BF16)
        h, xn, route = _merge(o_att, o_rwkv, gates, h, w_branch_att[l].astype(BF16),
                              w_branch_rwkv[l].astype(BF16), w_out[l].astype(BF16), norm_ffn[l],
                              wr_hi, wr_lo, b_route, lp)

        dispatch, pos = _dispatch(route, batch * (N_META + seq))
        y_rows = _moe(dispatch, xn, w_exp_gate, w_exp_up, w_exp_down, l)
        h = _combine(pos, y_rows, h, route, lp)

    return _final_norm(h, norm_final, batch, lp).reshape(batch, seq, d)
```
